```python
import jax, jax.numpy as jnp
from jax import lax
import numpy as np

D_MODEL = 2048
BATCH = 16
SEQ = 256
DEPTH = 4
DEC_BATCH = 8
DEC_SEQ = 4096
PAST_LEN = 256

GRID_W = 64
N_EVEN = (DEPTH + 1) // 2
N_ODD = DEPTH // 2
CONV_CH = D_MODEL // 2
CONV_K = 31
MLA_HEADS = 8
QK_NOPE = 128
QK_ROPE = 64
V_HEAD = 128
QK_HEAD = QK_NOPE + QK_ROPE
Q_LORA = 768
KV_LORA = 512
IN_COLS = 2 * CONV_CH + Q_LORA + KV_LORA + QK_ROPE
MIX_OUT = CONV_CH + MLA_HEADS * V_HEAD
ROPE_THETA = 10000.0
Q_BLOCK = 128
POOL_WINDOWS = (2, 4, 8, 16)
POOL_GROUPS = len(POOL_WINDOWS)
POOL_G = D_MODEL // POOL_GROUPS
D_FF = ((8 * D_MODEL + 3 * 256 - 1) // (3 * 256)) * 256
N_MOD = 6
EPS = 1e-6

kernel_name = 'hybrid_diffusion_prefix_trunk_step'


def rms_norm(x, w):
    xf = x.astype(jnp.float32)
    y = xf * lax.rsqrt(jnp.mean(xf * xf, axis=-1, keepdims=True) + EPS)
    return (y * w.astype(jnp.float32)).astype(x.dtype)


def layer_norm(x, w, b):
    xf = x.astype(jnp.float32)
    mu = jnp.mean(xf, axis=-1, keepdims=True)
    var = jnp.mean(jnp.square(xf - mu), axis=-1, keepdims=True)
    y = (xf - mu) * lax.rsqrt(var + EPS)
    return (y * w.astype(jnp.float32) + b.astype(jnp.float32)).astype(x.dtype)


def axial_rope_tables(rows):
    row = jnp.broadcast_to(jnp.arange(rows, dtype=jnp.float32)[:, None], (rows, GRID_W)).reshape(-1)
    col = jnp.broadcast_to(jnp.arange(GRID_W, dtype=jnp.float32)[None, :], (rows, GRID_W)).reshape(-1)
    n_freq = QK_ROPE // 4
    inv_freq = 1.0 / (ROPE_THETA ** (jnp.arange(n_freq, dtype=jnp.float32) / n_freq))
    ang = jnp.concatenate([row[:, None] * inv_freq, col[:, None] * inv_freq], axis=-1)
    return jnp.cos(ang), jnp.sin(ang)


def apply_rope(x, cos, sin):
    xf = x.astype(jnp.float32).reshape(x.shape[:-1] + (QK_ROPE // 2, 2))
    c, s = cos[:, None, :], sin[:, None, :]
    x0, x1 = xf[..., 0], xf[..., 1]
    y = jnp.stack([x0 * c - x1 * s, x0 * s + x1 * c], axis=-1)
    return y.reshape(x.shape).astype(x.dtype)


def rope_heads(x, cos, sin):
    return jnp.concatenate([x[..., :QK_NOPE], apply_rope(x[..., QK_NOPE:], cos, sin)], axis=-1)


def blocked_attention(q, k, v):
    b, sq, h, dk = q.shape
    nblk = sq // Q_BLOCK
    qb = q.reshape(b, nblk, Q_BLOCK, h, dk).transpose(1, 0, 2, 3, 4)
    scale = dk ** -0.5

    def one_block(q_blk):
        s = jnp.einsum('bqhd,bkhd->bhqk', q_blk, k, preferred_element_type=jnp.float32) * scale
        p = jax.nn.softmax(s, axis=-1).astype(v.dtype)
        return jnp.einsum('bhqk,bkhd->bqhd', p, v)

    out = lax.map(one_block, qb)
    return out.transpose(1, 0, 2, 3, 4).reshape(b, sq, h, v.shape[-1])


def mla_expand(ckv, kpe, w_kv_b, k_norm_w):
    b, s, _ = ckv.shape
    kv = jnp.einsum('bsr,rn->bsn', ckv, w_kv_b).reshape(b, s, MLA_HEADS, QK_NOPE + V_HEAD)
    k_nope, v = kv[..., :QK_NOPE], kv[..., QK_NOPE:]
    k_rope = jnp.broadcast_to(kpe[:, :, None, :], (b, s, MLA_HEADS, QK_ROPE))
    k = rms_norm(jnp.concatenate([k_nope, k_rope], axis=-1), k_norm_w)
    return k, v


def conformer_conv(u, conv_dw_w, conv_dw_b, conv_ln_w, conv_ln_b):
    a, g = u[..., :CONV_CH], u[..., CONV_CH:]
    h = a * jax.nn.sigmoid(g)
    h = lax.conv_general_dilated(h, conv_dw_w[:, None, :].astype(h.dtype), window_strides=(1,),
                                 padding=[(CONV_K // 2, CONV_K // 2)],
                                 dimension_numbers=('NWC', 'WIO', 'NWC'),
                                 feature_group_count=CONV_CH) + conv_dw_b
    return jax.nn.silu(layer_norm(h, conv_ln_w, conv_ln_b))


def even_mixer(h, w_in, conv_dw_w, conv_dw_b, conv_ln_w, conv_ln_b, q_a_norm_w, w_q_b,
               kv_a_norm_w, w_kv_b, q_norm_w, k_norm_w, w_out, ctx_ckv=None, ctx_kpe=None, rope=None):
    b, s, _ = h.shape
    proj = jnp.einsum('bsd,dn->bsn', h, w_in)
    o1 = 2 * CONV_CH
    o2 = o1 + Q_LORA
    o3 = o2 + KV_LORA
    u_conv = proj[..., :o1]
    q_a = proj[..., o1:o2]
    ckv = rms_norm(proj[..., o2:o3], kv_a_norm_w)
    kpe = proj[..., o3:]

    conv_out = conformer_conv(u_conv, conv_dw_w, conv_dw_b, conv_ln_w, conv_ln_b)

    q = jnp.einsum('bsr,rn->bsn', rms_norm(q_a, q_a_norm_w), w_q_b).reshape(b, s, MLA_HEADS, QK_HEAD)
    q = rms_norm(q, q_norm_w)
    k, v = mla_expand(ckv, kpe, w_kv_b, k_norm_w)
    if rope is not None:
        cos, sin = rope
        q = rope_heads(q, cos, sin)
        k = rope_heads(k, cos, sin)
        k_c, v_c = mla_expand(ctx_ckv, ctx_kpe, w_kv_b, k_norm_w)
        k = jnp.concatenate([k_c, k], axis=1)
        v = jnp.concatenate([v_c, v], axis=1)
    attn = blocked_attention(q, k, v).reshape(b, s, MLA_HEADS * V_HEAD)
    out = jnp.einsum('bsn,nd->bsd', jnp.concatenate([conv_out, attn], axis=-1), w_out)
    return out, ckv, kpe


def pool_mixer(h, pool_w, pool_scale):
    b, s, d = h.shape
    hf = h.astype(jnp.float32)
    cs = jnp.concatenate([jnp.zeros((b, 1, d), jnp.float32), jnp.cumsum(hf, axis=1)], axis=1)
    t = jnp.arange(s)
    groups = []
    for gi, w in enumerate(POOL_WINDOWS):
        lo_c, hi_c = gi * POOL_G, (gi + 1) * POOL_G
        lo = jnp.clip(t - w // 2, 0, s)
        hi = jnp.clip(t + w // 2, 0, s)
        csg = cs[..., lo_c:hi_c]
        mean = (jnp.take(csg, hi, axis=1) - jnp.take(csg, lo, axis=1)) / (hi - lo).astype(jnp.float32)[:, None]
        groups.append(mean - hf[..., lo_c:hi_c])
    mixed = jnp.stack(groups, axis=2).astype(h.dtype)
    y = jnp.einsum('bsgc,gcn->bsgn', mixed, pool_w).reshape(b, s, d)
    return y * pool_scale


def swiglu(h, w_gate, w_up, w_down):
    return jnp.einsum('bsf,fd->bsd', jax.nn.silu(jnp.einsum('bsd,df->bsf', h, w_gate)) * jnp.einsum('bsd,df->bsf', h, w_up), w_down)


def modulation(cond, w_mod, b_mod):
    m = jnp.einsum('...d,dn->...n', jax.nn.silu(cond), w_mod) + b_mod
    return jnp.split(m[..., None, :], N_MOD, axis=-1)


def setup_inputs(seed: int = 0) -> dict:
    key = jax.random.key(seed)
    ks = jax.random.split(key, 32)
    f32 = jnp.float32
    nrm = lambda k, shape, scale: jax.random.normal(k, shape, f32) * scale
    gain = lambda k, shape: 1.0 + 0.1 * jax.random.normal(k, shape, f32)
    return {
        'x_prompt': nrm(ks[0], (BATCH, SEQ, D_MODEL), 1.0),
        'x_sample': nrm(ks[1], (DEC_BATCH, DEC_SEQ, D_MODEL), 1.0),
        'cache_ckv': nrm(ks[2], (DEC_BATCH, N_EVEN, PAST_LEN, KV_LORA), 1.0),
        'cache_kpe': nrm(ks[3], (DEC_BATCH, N_EVEN, PAST_LEN, QK_ROPE), 1.0),
        'c': nrm(ks[4], (DEC_BATCH, D_MODEL), 1.0),
        'c_ctx': nrm(ks[5], (D_MODEL,), 1.0),
        'norm1_w': gain(ks[6], (DEPTH, D_MODEL)),
        'norm2_w': gain(ks[7], (DEPTH, D_MODEL)),
        'w_mod': nrm(ks[8], (DEPTH, D_MODEL, N_MOD * D_MODEL), D_MODEL ** -0.5),
        'b_mod': nrm(ks[9], (DEPTH, N_MOD * D_MODEL), 0.01),
        'w_in': nrm(ks[10], (N_EVEN, D_MODEL, IN_COLS), D_MODEL ** -0.5),
        'conv_dw_w': nrm(ks[11], (N_EVEN, CONV_K, CONV_CH), CONV_K ** -0.5),
        'conv_dw_b': nrm(ks[12], (N_EVEN, CONV_CH), 0.01),
        'conv_ln_w': gain(ks[13], (N_EVEN, CONV_CH)),
        'conv_ln_b': nrm(ks[14], (N_EVEN, CONV_CH), 0.01),
        'q_a_norm_w': gain(ks[15], (N_EVEN, Q_LORA)),
        'w_q_b': nrm(ks[16], (N_EVEN, Q_LORA, MLA_HEADS * QK_HEAD), Q_LORA ** -0.5),
        'kv_a_norm_w': gain(ks[17], (N_EVEN, KV_LORA)),
        'w_kv_b': nrm(ks[18], (N_EVEN, KV_LORA, MLA_HEADS * (QK_NOPE + V_HEAD)), KV_LORA ** -0.5),
        'q_norm_w': gain(ks[19], (N_EVEN, QK_HEAD)),
        'k_norm_w': gain(ks[20], (N_EVEN, QK_HEAD)),
        'w_out': nrm(ks[21], (N_EVEN, MIX_OUT, D_MODEL), MIX_OUT ** -0.5),
        'pool_w': nrm(ks[22], (N_ODD, POOL_GROUPS, POOL_G, POOL_G), POOL_G ** -0.5),
        'pool_scale': gain(ks[23], (N_ODD, D_MODEL)),
        'ffn_w_gate': nrm(ks[24], (DEPTH, D_MODEL, D_FF), D_MODEL ** -0.5),
        'ffn_w_up': nrm(ks[25], (DEPTH, D_MODEL, D_FF), D_MODEL ** -0.5),
        'ffn_w_down': nrm(ks[26], (DEPTH, D_FF, D_MODEL), D_FF ** -0.5),
    }


def reference(x_prompt, x_sample, cache_ckv, cache_kpe, c, c_ctx, norm1_w, norm2_w, w_mod, b_mod,
              w_in, conv_dw_w, conv_dw_b, conv_ln_w, conv_ln_b, q_a_norm_w, w_q_b, kv_a_norm_w, w_kv_b,
              q_norm_w, k_norm_w, w_out, pool_w, pool_scale, ffn_w_gate, ffn_w_up, ffn_w_down):
    rows = x_sample.shape[1] // GRID_W
    rope = axial_rope_tables(rows)
    xp, xs = x_prompt, x_sample
    new_ckv, new_kpe = [], []
    for layer in range(DEPTH):
        sh1_p, sc1_p, g1_p, sh2_p, sc2_p, g2_p = modulation(c_ctx, w_mod[layer], b_mod[layer])
        sh1_s, sc1_s, g1_s, sh2_s, sc2_s, g2_s = modulation(c, w_mod[layer], b_mod[layer])
        hp = rms_norm(xp, norm1_w[layer]) * (1.0 + sc1_p) + sh1_p
        hs = rms_norm(xs, norm1_w[layer]) * (1.0 + sc1_s) + sh1_s
        if layer % 2 == 0:
            e = layer // 2
            params = (w_in[e], conv_dw_w[e], conv_dw_b[e], conv_ln_w[e], conv_ln_b[e], q_a_norm_w[e], w_q_b[e],
                      kv_a_norm_w[e], w_kv_b[e], q_norm_w[e], k_norm_w[e], w_out[e])
            mp, ckv_p, kpe_p = even_mixer(hp, *params)
            new_ckv.append(ckv_p)
            new_kpe.append(kpe_p)
            ms, _, _ = even_mixer(hs, *params, ctx_ckv=cache_ckv[:, e], ctx_kpe=cache_kpe[:, e], rope=rope)
        else:
            o = layer // 2
            mp = pool_mixer(hp, pool_w[o], pool_scale[o])
            ms = pool_mixer(hs, pool_w[o], pool_scale[o])
        xp = xp + g1_p * mp
        xs = xs + g1_s * ms
        hp = rms_norm(xp, norm2_w[layer]) * (1.0 + sc2_p) + sh2_p
        hs = rms_norm(xs, norm2_w[layer]) * (1.0 + sc2_s) + sh2_s
        xp = xp + g2_p * swiglu(hp, ffn_w_gate[layer], ffn_w_up[layer], ffn_w_down[layer])
        xs = xs + g2_s * swiglu(hs, ffn_w_gate[layer], ffn_w_up[layer], ffn_w_down[layer])
    new_ckv_arr = jnp.stack(new_ckv, axis=1)
    new_kpe_arr = jnp.stack(new_kpe, axis=1)
    return (xp, xs, new_ckv_arr, new_kpe_arr)
```

```python
import functools
import math

import jax
import jax.numpy as jnp
from jax import lax
from jax.experimental import pallas as pl
from jax.experimental.pallas import tpu as pltpu

F32 = jnp.float32
BF16 = jnp.bfloat16

EPS = 1e-6
GRID_W = 64
CONV_K = 31
MLA_HEADS = 8
QK_NOPE = 128
QK_ROPE = 64
V_HEAD = 128
QK_HEAD = QK_NOPE + QK_ROPE
Q_LORA = 768
KV_LORA = 512
ROPE_THETA = 10000.0
POOL_WINDOWS = (2, 4, 8, 16)
N_MOD = 6

LANES = 128
HEAD_PAD = 2 * LANES
ROPE_HALF = QK_ROPE // 2
MOD_ROWS = 16
CONV_HALO = 16
POOL_HALO = 8
VMEM_LIMIT = 56 * 1024 * 1024


def _cparams(sem, vmem=VMEM_LIMIT):
    return pltpu.CompilerParams(dimension_semantics=sem, vmem_limit_bytes=vmem)


def _const_spec(shape):
    nd = len(shape)
    return pl.BlockSpec(shape, lambda *_: (0,) * nd, pipeline_mode=pl.Buffered(1))


def _rms(x, w):
    ms = jnp.mean(x * x, axis=-1, keepdims=True)
    return x * lax.rsqrt(ms + EPS) * w


def _silu(x):
    return x * jax.nn.sigmoid(x)


def _pad_rope(w):
    z = jnp.zeros(w.shape[:-1] + (ROPE_HALF,), w.dtype)
    return jnp.concatenate([w[..., 0::2], z, w[..., 1::2], z], axis=-1)


def _unpad_rope(w):
    return jnp.stack([w[..., 0:ROPE_HALF], w[..., 2 * ROPE_HALF:3 * ROPE_HALF]], axis=-1).reshape(
        w.shape[:-1] + (QK_ROPE,))


class _Geom:
    def __init__(self, dec_b, dec_s, b, s, d):
        self.dec_b, self.dec_s, self.b, self.s, self.d = dec_b, dec_s, b, s, d
        self.ns = dec_b * dec_s
        self.np_ = b * s
        self.t = self.ns + self.np_

    def mod_row(self, layer, tile):
        def f(i):
            tok = i * tile
            return layer * MOD_ROWS + jnp.where(tok < self.ns, tok // self.dec_s, self.dec_b)
        return f


def _mod_body(c_ref, w_ref, b_ref, o_ref):
    s = _silu(c_ref[...]).astype(BF16)
    w = w_ref[0].astype(BF16)
    o_ref[0] = jnp.dot(s, w, preferred_element_type=F32) + b_ref[0]


def _modulation(cond, w_mod, b_mod):
    depth, d, n = w_mod.shape
    tn = 1024
    return pl.pallas_call(
        _mod_body,
        grid=(depth, n // tn),
        in_specs=[
            pl.BlockSpec((MOD_ROWS, d), lambda l, j: (0, 0)),
            pl.BlockSpec((1, d, tn), lambda l, j: (l, 0, j)),
            pl.BlockSpec((1, 1, tn), lambda l, j: (l, 0, j)),
        ],
        out_specs=pl.BlockSpec((1, MOD_ROWS, tn), lambda l, j: (l, 0, j)),
        out_shape=jax.ShapeDtypeStruct((depth, MOD_ROWS, n), F32),
        compiler_params=_cparams(("arbitrary", "arbitrary")),
        name="modulation",
    )(cond, w_mod, b_mod.reshape(depth, 1, n))


def _rope128(x, c, s):
    return x * c + pltpu.roll(x, 2 * ROPE_HALF, 1) * s


def _assemble_kv(kv, kpe, knw, rope, k_ref, v_ref):
    hv = MLA_HEADS * QK_NOPE
    kpe_ss = jnp.sum(kpe * kpe, axis=-1, keepdims=True)
    kpe_w = kpe * knw[:, QK_NOPE:]
    if rope is not None:
        kpe_w = _rope128(kpe_w, rope[0], rope[1])
    for h in range(MLA_HEADS):
        kn = kv[:, h * QK_NOPE:(h + 1) * QK_NOPE]
        ss = jnp.sum(kn * kn, axis=-1, keepdims=True) + kpe_ss
        r = lax.rsqrt(ss * (1.0 / QK_HEAD) + EPS)
        k_ref[:, h * HEAD_PAD:h * HEAD_PAD + QK_NOPE] = (kn * r * knw[:, :QK_NOPE]).astype(BF16)
        k_ref[:, h * HEAD_PAD + QK_NOPE:(h + 1) * HEAD_PAD] = (kpe_w * r).astype(BF16)
    v_ref[...] = kv[:, hv:].astype(BF16)


def _proj_body(d, cc, x_ref, mod_ref, n1w_ref, win_ref, qanw_ref, wqb_ref, kvanw_ref, wkvb_ref,
               qnw_ref, knw_ref, rc_ref, rs_ref,
               glu_ref, q_ref, k_ref, v_ref, ckv_ref, kpe_ref):
    m = mod_ref[0]
    h = _rms(x_ref[...], n1w_ref[...]) * (1.0 + m[:, d:2 * d]) + m[:, 0:d]
    proj = jnp.dot(h.astype(BF16), win_ref[...], preferred_element_type=F32)
    o1 = 2 * cc
    o2 = o1 + Q_LORA
    o3 = o2 + KV_LORA
    glu_ref[...] = proj[:, :cc] * jax.nn.sigmoid(proj[:, cc:o1])

    rc = rc_ref[...]
    rs = rs_ref[...]
    qa = _rms(proj[:, o1:o2], qanw_ref[...]).astype(BF16)
    qb = jnp.dot(qa, wqb_ref[...], preferred_element_type=F32)
    qnw = qnw_ref[...]
    for hd in range(MLA_HEADS):
        qh = qb[:, hd * HEAD_PAD:(hd + 1) * HEAD_PAD]
        ss = jnp.sum(qh * qh, axis=-1, keepdims=True)
        r = lax.rsqrt(ss * (1.0 / QK_HEAD) + EPS)
        qn = qh * r * qnw
        q_ref[:, hd * HEAD_PAD:hd * HEAD_PAD + QK_NOPE] = qn[:, :QK_NOPE].astype(BF16)
        q_ref[:, hd * HEAD_PAD + QK_NOPE:(hd + 1) * HEAD_PAD] = _rope128(qn[:, QK_NOPE:], rc, rs).astype(BF16)

    ckv = _rms(proj[:, o2:o3], kvanw_ref[...])
    ckv_ref[...] = ckv
    kpe = proj[:, o3:]
    kpe_ref[...] = kpe
    kv = jnp.dot(ckv.astype(BF16), wkvb_ref[...], preferred_element_type=F32)
    _assemble_kv(kv, kpe, knw_ref[...], (rc, rs), k_ref, v_ref)


def _proj(g, layer, x, mod, n1w, win, qanw, wqb, kvanw, wkvb, qnw, knw, rope_c, rope_s, tm):
    d = g.d
    cc = d // 2
    nin = win.shape[1]
    nq = MLA_HEADS * HEAD_PAD
    nv = MLA_HEADS * V_HEAD
    s_tiles = g.dec_s // tm

    def rope_idx(i):
        return (jnp.where(i * tm < g.ns, i % s_tiles, s_tiles), 0)

    row = lambda n: pl.BlockSpec((tm, n), lambda i: (i, 0))
    mrow = g.mod_row(layer, tm)
    return pl.pallas_call(
        functools.partial(_proj_body, d, cc),
        grid=(g.t // tm,),
        in_specs=[
            row(d),
            pl.BlockSpec((1, 1, N_MOD * d), lambda i: (mrow(i), 0, 0)),
            _const_spec((1, d)),
            _const_spec((d, nin)),
            _const_spec((1, Q_LORA)),
            _const_spec((Q_LORA, nq)),
            _const_spec((1, KV_LORA)),
            _const_spec((KV_LORA, nq)),
            _const_spec((1, HEAD_PAD)),
            _const_spec((1, HEAD_PAD)),
            pl.BlockSpec((tm, LANES), rope_idx),
            pl.BlockSpec((tm, LANES), rope_idx),
        ],
        out_specs=[row(cc), row(nq), row(nq), row(nv), row(KV_LORA), row(LANES)],
        out_shape=[
            jax.ShapeDtypeStruct((g.t, cc), F32),
            jax.ShapeDtypeStruct((g.t, nq), BF16),
            jax.ShapeDtypeStruct((g.t, nq), BF16),
            jax.ShapeDtypeStruct((g.t, nv), BF16),
            jax.ShapeDtypeStruct((g.t, KV_LORA), F32),
            jax.ShapeDtypeStruct((g.t, LANES), F32),
        ],
        compiler_params=_cparams(("arbitrary",)),
        name="even_proj",
    )(x, mod, n1w, win, qanw, wqb, kvanw, wkvb, qnw, knw, rope_c, rope_s)


def _ctx_kv_body(ckv_ref, kpe_ref, wkvb_ref, knw_ref, k_ref, v_ref):
    kv = jnp.dot(ckv_ref[...], wkvb_ref[...], preferred_element_type=F32)
    _assemble_kv(kv, kpe_ref[...], knw_ref[...], None, k_ref, v_ref)


def _ctx_kv(ckv, kpe, wkvb, knw, tm):
    n = ckv.shape[0]
    nq = MLA_HEADS * HEAD_PAD
    nv = MLA_HEADS * V_HEAD
    row = lambda w: pl.BlockSpec((tm, w), lambda i: (i, 0))
    return pl.pallas_call(
        _ctx_kv_body,
        grid=(n // tm,),
        in_specs=[row(KV_LORA), row(LANES), _const_spec((KV_LORA, nq)), _const_spec((1, HEAD_PAD))],
        out_specs=[row(nq), row(nv)],
        out_shape=[jax.ShapeDtypeStruct((n, nq), BF16), jax.ShapeDtypeStruct((n, nv), BF16)],
        compiler_params=_cparams(("arbitrary",)),
        name="ctx_kv",
    )(ckv, kpe, wkvb, knw)


_NT = (((1,), (1,)), ((), ()))
_EXP2_SCALE = math.log2(math.e) * QK_HEAD ** -0.5


def _attn_sample_body(q_ref, kc_ref, kl_ref, vc_ref, vl_ref, o_ref):
    q = q_ref[...]
    s1 = lax.dot_general(q, kc_ref[...], _NT, preferred_element_type=F32)
    s2 = lax.dot_general(q, kl_ref[...], _NT, preferred_element_type=F32)
    m = jnp.maximum(jnp.max(s1, axis=-1, keepdims=True), jnp.max(s2, axis=-1, keepdims=True))
    p1 = jnp.exp2((s1 - m) * _EXP2_SCALE)
    p2 = jnp.exp2((s2 - m) * _EXP2_SCALE)
    l = jnp.sum(p1, axis=-1, keepdims=True) + jnp.sum(p2, axis=-1, keepdims=True)
    o = jnp.dot(p1.astype(BF16), vc_ref[...], preferred_element_type=F32)
    o = o + jnp.dot(p2.astype(BF16), vl_ref[...], preferred_element_type=F32)
    o_ref[...] = (o / l).astype(BF16)


def _attn_sample(g, q, k, v, kc, vc, past, tq):
    nqt = g.dec_s // tq
    return pl.pallas_call(
        _attn_sample_body,
        grid=(g.dec_b, MLA_HEADS, nqt),
        in_specs=[
            pl.BlockSpec((tq, HEAD_PAD), lambda b, h, i: (b * nqt + i, h)),
            pl.BlockSpec((past, HEAD_PAD), lambda b, h, i: (b, h)),
            pl.BlockSpec((g.dec_s, HEAD_PAD), lambda b, h, i: (b, h)),
            pl.BlockSpec((past, V_HEAD), lambda b, h, i: (b, h)),
            pl.BlockSpec((g.dec_s, V_HEAD), lambda b, h, i: (b, h)),
        ],
        out_specs=pl.BlockSpec((tq, V_HEAD), lambda b, h, i: (b * nqt + i, h)),
        out_shape=jax.ShapeDtypeStruct((g.ns, MLA_HEADS * V_HEAD), BF16),
        compiler_params=_cparams(("arbitrary", "arbitrary", "arbitrary")),
        name="attn_sample",
    )(q, kc, k, vc, v)


def _attn_prompt_body(q_ref, k_ref, v_ref, o_ref):
    s = lax.dot_general(q_ref[...], k_ref[...], _NT, preferred_element_type=F32)
    m = jnp.max(s, axis=-1, keepdims=True)
    p = jnp.exp2((s - m) * _EXP2_SCALE)
    l = jnp.sum(p, axis=-1, keepdims=True)
    o = jnp.dot(p.astype(BF16), v_ref[...], preferred_element_type=F32)
    o_ref[...] = (o / l).astype(BF16)


def _attn_prompt(g, q, k, v):
    off = g.ns // g.s
    return pl.pallas_call(
        _attn_prompt_body,
        grid=(g.b, MLA_HEADS),
        in_specs=[
            pl.BlockSpec((g.s, HEAD_PAD), lambda b, h: (off + b, h)),
            pl.BlockSpec((g.s, HEAD_PAD), lambda b, h: (off + b, h)),
            pl.BlockSpec((g.s, V_HEAD), lambda b, h: (off + b, h)),
        ],
        out_specs=pl.BlockSpec((g.s, V_HEAD), lambda b, h: (b, h)),
        out_shape=jax.ShapeDtypeStruct((g.np_, MLA_HEADS * V_HEAD), BF16),
        compiler_params=_cparams(("arbitrary", "arbitrary")),
        name="attn_prompt",
    )(q, k, v)


def _chunk_pos(g, ch):
    n = pl.program_id(0)
    n_s = g.ns // ch
    cps_s = g.dec_s // ch
    cps_p = g.s // ch
    is_s = n < n_s
    j = jnp.where(is_s, n % cps_s, (n - n_s) % cps_p)
    cps = jnp.where(is_s, cps_s, cps_p)
    return j, cps


def _halo_specs(g, ch, halo, width):
    per = ch // halo
    last = g.t // halo - 1
    main = pl.BlockSpec((ch, width), lambda n: (n, 0))
    left = pl.BlockSpec((halo, width), lambda n: (jnp.maximum(n * per - 1, 0), 0))
    right = pl.BlockSpec((halo, width), lambda n: (jnp.minimum((n + 1) * per, last), 0))
    return main, left, right


CONV_ROWS = 32


def _conv_body(g, ch, x_ref, xl_ref, xr_ref, w_ref, b_ref, lnw_ref, lnb_ref, o_ref, xp_ref):
    j, cps = _chunk_pos(g, ch)
    xp_ref[0:CONV_HALO] = jnp.where(j > 0, xl_ref[...], 0.0)
    xp_ref[CONV_HALO:CONV_HALO + ch] = x_ref[...]
    xp_ref[CONV_HALO + ch:2 * CONV_HALO + ch] = jnp.where(j < cps - 1, xr_ref[...], 0.0)
    bias = b_ref[...]
    lnw = lnw_ref[...]
    lnb = lnb_ref[...]
    base = CONV_HALO - CONV_K // 2

    for r0 in range(0, ch, CONV_ROWS):
        acc = xp_ref[r0 + base:r0 + base + CONV_ROWS, :] * w_ref[0:1, :]
        for k in range(1, CONV_K):
            acc = acc + xp_ref[r0 + base + k:r0 + base + k + CONV_ROWS, :] * w_ref[k:k + 1, :]
        acc = acc + bias
        mu = jnp.mean(acc, axis=-1, keepdims=True)
        xc = acc - mu
        var = jnp.mean(xc * xc, axis=-1, keepdims=True)
        y = xc * lax.rsqrt(var + EPS) * lnw + lnb
        o_ref[r0:r0 + CONV_ROWS, :] = _silu(y).astype(BF16)


def _conv(g, glu, w, b, lnw, lnb, ch):
    cc = glu.shape[1]
    main, left, right = _halo_specs(g, ch, CONV_HALO, cc)
    return pl.pallas_call(
        functools.partial(_conv_body, g, ch),
        grid=(g.t // ch,),
        in_specs=[main, left, right, _const_spec((CONV_K, cc)), _const_spec((1, cc)),
                  _const_spec((1, cc)), _const_spec((1, cc))],
        out_specs=pl.BlockSpec((ch, cc), lambda n: (n, 0)),
        out_shape=jax.ShapeDtypeStruct((g.t, cc), BF16),
        scratch_shapes=[pltpu.VMEM((ch + 2 * CONV_HALO, cc), F32)],
        compiler_params=_cparams(("arbitrary",)),
        name="conv_module",
    )(glu, glu, glu, w, b, lnw, lnb)


def _mix_out_body(d, cc, n_s_tiles, x_ref, mod_ref, ca_ref, as_ref, ap_ref, wo_ref, o_ref):
    i = pl.program_id(0)
    attn = jnp.where(i < n_s_tiles, as_ref[...], ap_ref[...])
    y = jnp.dot(ca_ref[...], wo_ref[0:cc, :], preferred_element_type=F32)
    y = y + jnp.dot(attn, wo_ref[cc:, :], preferred_element_type=F32)
    g1 = mod_ref[0][:, 2 * d:3 * d]
    o_ref[...] = x_ref[...] + g1 * y


def _mix_out(g, layer, x, mod, cact, attn_s, attn_p, wo, tm):
    d = g.d
    cc = cact.shape[1]
    na = attn_s.shape[1]
    n_s_tiles = g.ns // tm
    mrow = g.mod_row(layer, tm)
    return pl.pallas_call(
        functools.partial(_mix_out_body, d, cc, n_s_tiles),
        grid=(g.t // tm,),
        in_specs=[
            pl.BlockSpec((tm, d), lambda i: (i, 0)),
            pl.BlockSpec((1, 1, N_MOD * d), lambda i: (mrow(i), 0, 0)),
            pl.BlockSpec((tm, cc), lambda i: (i, 0)),
            pl.BlockSpec((tm, na), lambda i: (jnp.minimum(i, n_s_tiles - 1), 0)),
            pl.BlockSpec((tm, na), lambda i: (jnp.maximum(i - n_s_tiles, 0), 0)),
            _const_spec((cc + na, d)),
        ],
        out_specs=pl.BlockSpec((tm, d), lambda i: (i, 0)),
        out_shape=jax.ShapeDtypeStruct((g.t, d), F32),
        compiler_params=_cparams(("arbitrary",)),
        name="even_out",
    )(x, mod, cact, attn_s, attn_p, wo)


POOL_ROWS = 32


def _pool_body(g, ch, x_ref, xl_ref, xr_ref, mod_ref, n1w_ref, pw_ref, ps_ref, o_ref, hb_ref, mx_ref):
    d = g.d
    pg = d // len(POOL_WINDOWS)
    j, cps = _chunk_pos(g, ch)
    m = mod_ref[0]
    n1w = n1w_ref[...]
    scale = 1.0 + m[:, d:2 * d]
    shift = m[:, 0:d]

    def norm(x):
        return _rms(x, n1w) * scale + shift

    hb_ref[0:POOL_HALO] = jnp.where(j > 0, norm(xl_ref[...]), 0.0)
    hb_ref[POOL_HALO:POOL_HALO + ch] = norm(x_ref[...])
    hb_ref[POOL_HALO + ch:2 * POOL_HALO + ch] = jnp.where(j < cps - 1, norm(xr_ref[...]), 0.0)
    seq_len = cps * ch

    for r0 in range(0, ch, POOL_ROWS):
        pos = j * ch + r0 + lax.broadcasted_iota(jnp.int32, (POOL_ROWS, 1), 0)
        for gi, w in enumerate(POOL_WINDOWS):
            cols = slice(gi * pg, (gi + 1) * pg)
            lo = r0 + POOL_HALO - w // 2
            acc = hb_ref[lo:lo + POOL_ROWS, cols]
            for dd in range(1, w):
                acc = acc + hb_ref[lo + dd:lo + dd + POOL_ROWS, cols]
            cnt = jnp.minimum(pos + w // 2, seq_len) - jnp.maximum(pos - w // 2, 0)
            ctr = hb_ref[r0 + POOL_HALO:r0 + POOL_HALO + POOL_ROWS, cols]
            mx_ref[r0:r0 + POOL_ROWS, cols] = (acc / cnt.astype(F32) - ctr).astype(BF16)

    g1 = m[:, 2 * d:3 * d]
    ps = ps_ref[...]
    for gi in range(len(POOL_WINDOWS)):
        cols = slice(gi * pg, (gi + 1) * pg)
        y = jnp.dot(mx_ref[:, cols], pw_ref[gi], preferred_element_type=F32)
        o_ref[:, cols] = x_ref[:, cols] + g1[:, cols] * (y * ps[:, cols])


def _pool(g, layer, x, mod, n1w, pw, ps, ch):
    d = g.d
    pg = d // len(POOL_WINDOWS)
    main, left, right = _halo_specs(g, ch, POOL_HALO, d)
    mrow = g.mod_row(layer, ch)
    return pl.pallas_call(
        functools.partial(_pool_body, g, ch),
        grid=(g.t // ch,),
        in_specs=[main, left, right,
                  pl.BlockSpec((1, 1, N_MOD * d), lambda n: (mrow(n), 0, 0)),
                  _const_spec((1, d)),
                  _const_spec((len(POOL_WINDOWS), pg, pg)),
                  _const_spec((1, d))],
        out_specs=pl.BlockSpec((ch, d), lambda n: (n, 0)),
        out_shape=jax.ShapeDtypeStruct((g.t, d), F32),
        scratch_shapes=[pltpu.VMEM((ch + 2 * POOL_HALO, d), F32), pltpu.VMEM((ch, d), BF16)],
        compiler_params=_cparams(("arbitrary",)),
        name="pool_mixer",
    )(x, x, x, mod, n1w, pw, ps)


def _ffn_body(d, nj, x_ref, mod_ref, n2w_ref, wg_ref, wu_ref, wd_ref, o_ref, h_ref):
    j = pl.program_id(1)

    @pl.when(j == 0)
    def _():
        m = mod_ref[0]
        h = _rms(x_ref[...], n2w_ref[...]) * (1.0 + m[:, 4 * d:5 * d]) + m[:, 3 * d:4 * d]
        h_ref[...] = h.astype(BF16)
        o_ref[...] = jnp.zeros_like(o_ref)

    h = h_ref[...]
    gate = jnp.dot(h, wg_ref[...], preferred_element_type=F32)
    up = jnp.dot(h, wu_ref[...], preferred_element_type=F32)
    a = (_silu(gate) * up).astype(BF16)
    o_ref[...] += jnp.dot(a, wd_ref[...], preferred_element_type=F32)

    @pl.when(j == nj - 1)
    def _():
        g2 = mod_ref[0][:, 5 * d:6 * d]
        o_ref[...] = x_ref[...] + g2 * o_ref[...]


def _ffn(g, layer, x, mod, n2w, wg, wu, wd, tm, tf):
    d = g.d
    dff = wg.shape[1]
    nj = dff // tf
    mrow = g.mod_row(layer, tm)
    return pl.pallas_call(
        functools.partial(_ffn_body, d, nj),
        grid=(g.t // tm, nj),
        in_specs=[
            pl.BlockSpec((tm, d), lambda i, j: (i, 0)),
            pl.BlockSpec((1, 1, N_MOD * d), lambda i, j: (mrow(i), 0, 0)),
            pl.BlockSpec((1, d), lambda i, j: (0, 0)),
            pl.BlockSpec((d, tf), lambda i, j: (0, j)),
            pl.BlockSpec((d, tf), lambda i, j: (0, j)),
            pl.BlockSpec((tf, d), lambda i, j: (j, 0)),
        ],
        out_specs=pl.BlockSpec((tm, d), lambda i, j: (i, 0)),
        out_shape=jax.ShapeDtypeStruct((g.t, d), F32),
        scratch_shapes=[pltpu.VMEM((tm, d), BF16)],
        compiler_params=_cparams(("arbitrary", "arbitrary")),
        name="ffn",
    )(x, mod, n2w, wg, wu, wd)


def _rope_tables(g, tm):
    rows = g.dec_s // GRID_W
    row = jnp.broadcast_to(jnp.arange(rows, dtype=F32)[:, None], (rows, GRID_W)).reshape(-1)
    col = jnp.broadcast_to(jnp.arange(GRID_W, dtype=F32)[None, :], (rows, GRID_W)).reshape(-1)
    n_freq = QK_ROPE // 4
    inv_freq = 1.0 / (ROPE_THETA ** (jnp.arange(n_freq, dtype=F32) / n_freq))
    ang = jnp.concatenate([row[:, None] * inv_freq, col[:, None] * inv_freq], axis=-1)
    cos, sin = jnp.cos(ang), jnp.sin(ang)
    z = jnp.zeros_like(cos)
    c = jnp.concatenate([cos, z, cos, z], axis=-1)
    s = jnp.concatenate([-sin, z, sin, z], axis=-1)
    c = jnp.concatenate([c, jnp.ones((tm, LANES), F32)], axis=0)
    s = jnp.concatenate([s, jnp.zeros((tm, LANES), F32)], axis=0)
    return c, s


def kernel(x_prompt, x_sample, cache_ckv, cache_kpe, c, c_ctx, norm1_w, norm2_w, w_mod, b_mod, w_in, conv_dw_w, conv_dw_b, conv_ln_w, conv_ln_b, q_a_norm_w, w_q_b, kv_a_norm_w, w_kv_b, q_norm_w, k_norm_w, w_out, pool_w, pool_scale, ffn_w_gate, ffn_w_up, ffn_w_down):
    b, s, d = x_prompt.shape
    dec_b, dec_s, _ = x_sample.shape
    past = cache_ckv.shape[2]
    depth = norm1_w.shape[0]
    g = _Geom(dec_b, dec_s, b, s, d)
    cc = d // 2
    assert dec_b + 1 <= MOD_ROWS and g.ns % s == 0 and dec_s % GRID_W == 0

    tm_proj = min(256, s)
    tm_out = min(512, g.np_, dec_s)
    tm_ffn = min(512, g.np_, dec_s)
    tf = 512
    ch = s
    tq = min(256, dec_s)

    cond = jnp.concatenate([c, c_ctx[None, :], jnp.zeros((MOD_ROWS - dec_b - 1, d), F32)], axis=0)
    mod = _modulation(cond, w_mod, b_mod).reshape(depth * MOD_ROWS, 1, N_MOD * d)

    rope_c, rope_s = _rope_tables(g, tm_proj)
    x = jnp.concatenate([x_sample.reshape(g.ns, d), x_prompt.reshape(g.np_, d)], axis=0)

    new_ckv, new_kpe = [], []
    for layer in range(depth):
        n1w = norm1_w[layer][None, :]
        if layer % 2 == 0:
            e = layer // 2
            win = jnp.concatenate([w_in[e][:, :2 * cc + Q_LORA + KV_LORA],
                                   _pad_rope(w_in[e][:, 2 * cc + Q_LORA + KV_LORA:])], axis=-1).astype(BF16)
            wq = w_q_b[e].reshape(Q_LORA, MLA_HEADS, QK_HEAD)
            wqb = jnp.concatenate([wq[..., :QK_NOPE], _pad_rope(wq[..., QK_NOPE:])], axis=-1)
            wqb = wqb.reshape(Q_LORA, MLA_HEADS * HEAD_PAD).astype(BF16)
            wkv = w_kv_b[e].reshape(KV_LORA, MLA_HEADS, QK_NOPE + V_HEAD)
            wkvb = jnp.concatenate([wkv[..., :QK_NOPE].reshape(KV_LORA, -1),
                                    wkv[..., QK_NOPE:].reshape(KV_LORA, -1)], axis=-1).astype(BF16)
            qnw = jnp.concatenate([q_norm_w[e][:QK_NOPE], _pad_rope(q_norm_w[e][QK_NOPE:])])[None, :]
            knw = jnp.concatenate([k_norm_w[e][:QK_NOPE], _pad_rope(k_norm_w[e][QK_NOPE:])])[None, :]

            glu, q, k, v, ckv, kpe = _proj(
                g, layer, x, mod, n1w, win, q_a_norm_w[e][None, :], wqb, kv_a_norm_w[e][None, :], wkvb,
                qnw, knw, rope_c, rope_s, tm_proj)
            new_ckv.append(ckv[g.ns:].reshape(b, s, KV_LORA))
            new_kpe.append(_unpad_rope(kpe[g.ns:]).reshape(b, s, QK_ROPE))

            kc, vc = _ctx_kv(cache_ckv[:, e].reshape(dec_b * past, KV_LORA).astype(BF16),
                             _pad_rope(cache_kpe[:, e]).reshape(dec_b * past, LANES), wkvb, knw, past)
            attn_s = _attn_sample(g, q, k, v, kc, vc, past, tq)
            attn_p = _attn_prompt(g, q, k, v)
            cact = _conv(g, glu, conv_dw_w[e], conv_dw_b[e][None, :], conv_ln_w[e][None, :],
                         conv_ln_b[e][None, :], ch)
            x = _mix_out(g, layer, x, mod, cact, attn_s, attn_p, w_out[e].astype(BF16), tm_out)
        else:
            o = layer // 2
            x = _pool(g, layer, x, mod, n1w, pool_w[o].astype(BF16), pool_scale[o][None, :], ch)
        x = _ffn(g, layer, x, mod, norm2_w[layer][None, :], ffn_w_gate[layer].astype(BF16),
                 ffn_w_up[layer].astype(BF16), ffn_w_down[layer].astype(BF16), tm_ffn, tf)

    y_sample = x[:g.ns].reshape(dec_b, dec_s, d)
    y_prompt = x[g.ns:].reshape(b, s, d)
    return (y_prompt, y_sample, jnp.stack(new_ckv, axis=1), jnp.stack(new_kpe, axis=1))
```

```python
import functools
import math

import jax
import jax.numpy as jnp
from jax import lax
from jax.experimental import pallas as pl
from jax.experimental.pallas import tpu as pltpu

F32 = jnp.float32
BF16 = jnp.bfloat16

EPS = 1e-6
GRID_W = 64
CONV_K = 31
MLA_HEADS = 8
QK_NOPE = 128
QK_ROPE = 64
V_HEAD = 128
QK_HEAD = QK_NOPE + QK_ROPE
Q_LORA = 768
KV_LORA = 512
ROPE_THETA = 10000.0
POOL_WINDOWS = (2, 4, 8, 16)
N_MOD = 6

LANES = 128
SUBLANES = 8
HEAD_PAD = 2 * LANES
ROPE_HALF = QK_ROPE // 2
MOD_ROWS = 16
CONV_HALO = 16
POOL_HALO = 8
VMEM_LIMIT = 62 * 1024 * 1024


def _cparams(sem, vmem=VMEM_LIMIT):
    return pltpu.CompilerParams(dimension_semantics=sem, vmem_limit_bytes=vmem)


def _const_spec(shape):
    nd = len(shape)
    return pl.BlockSpec(shape, lambda *_: (0,) * nd, pipeline_mode=pl.Buffered(1))


def _rms(x, w):
    ms = jnp.mean(x * x, axis=-1, keepdims=True)
    return x * lax.rsqrt(ms + EPS) * w


def _silu(x):
    return x * jax.nn.sigmoid(x)


def _pad_rope(w):
    z = jnp.zeros(w.shape[:-1] + (ROPE_HALF,), w.dtype)
    return jnp.concatenate([w[..., 0::2], z, w[..., 1::2], z], axis=-1)


def _unpad_rope(w):
    return jnp.stack([w[..., 0:ROPE_HALF], w[..., 2 * ROPE_HALF:3 * ROPE_HALF]], axis=-1).reshape(
        w.shape[:-1] + (QK_ROPE,))


class _Geom:
    def __init__(self, dec_b, dec_s, b, s, d):
        self.dec_b, self.dec_s, self.b, self.s, self.d = dec_b, dec_s, b, s, d
        self.ns = dec_b * dec_s
        self.np_ = b * s
        self.t = self.ns + self.np_

    def mod_row(self, layer, tile):
        def f(i):
            tok = i * tile
            return layer * MOD_ROWS + jnp.where(tok < self.ns, tok // self.dec_s, self.dec_b)
        return f


def _x_in(g, tm, x, idx=lambda i: i):
    d = g.d
    if not isinstance(x, tuple):
        return [pl.BlockSpec((tm, d), lambda *a: (idx(*a), 0))], (x,)
    n_s = g.ns // tm
    return [pl.BlockSpec((tm, d), lambda *a: (jnp.minimum(idx(*a), n_s - 1), 0)),
            pl.BlockSpec((tm, d), lambda *a: (jnp.maximum(idx(*a) - n_s, 0), 0))], x


def _x_load(x_refs, is_sample):
    if len(x_refs) == 1:
        return x_refs[0][...]
    return jnp.where(is_sample, x_refs[0][...], x_refs[1][...])


def _mod_body(c_ref, w_ref, b_ref, o_ref):
    s = _silu(c_ref[...]).astype(BF16)
    w = w_ref[0].astype(BF16)
    o_ref[0] = jnp.dot(s, w, preferred_element_type=F32) + b_ref[0]


def _modulation(cond, w_mod, b_mod):
    depth, d, n = w_mod.shape
    tn = 1024
    return pl.pallas_call(
        _mod_body,
        grid=(depth, n // tn),
        in_specs=[
            pl.BlockSpec((MOD_ROWS, d), lambda l, j: (0, 0)),
            pl.BlockSpec((1, d, tn), lambda l, j: (l, 0, j)),
            pl.BlockSpec((1, 1, tn), lambda l, j: (l, 0, j)),
        ],
        out_specs=pl.BlockSpec((1, MOD_ROWS, tn), lambda l, j: (l, 0, j)),
        out_shape=jax.ShapeDtypeStruct((depth, MOD_ROWS, n), F32),
        compiler_params=_cparams(("arbitrary", "arbitrary")),
        name="modulation",
    )(cond, w_mod, b_mod.reshape(depth, 1, n))


def _rope128(x, c, s):
    return x * c + pltpu.roll(x, 2 * ROPE_HALF, 1) * s


def _assemble_kv(kv, kpe, knw, rope, k_ref, v_ref):
    hv = MLA_HEADS * QK_NOPE
    kpe_ss = jnp.sum(kpe * kpe, axis=-1, keepdims=True)
    kpe_w = kpe * knw[:, QK_NOPE:]
    if rope is not None:
        kpe_w = _rope128(kpe_w, rope[0], rope[1])
    for h in range(MLA_HEADS):
        kn = kv[:, h * QK_NOPE:(h + 1) * QK_NOPE]
        ss = jnp.sum(kn * kn, axis=-1, keepdims=True) + kpe_ss
        r = lax.rsqrt(ss * (1.0 / QK_HEAD) + EPS)
        k_ref[:, h * HEAD_PAD:h * HEAD_PAD + QK_NOPE] = (kn * r * knw[:, :QK_NOPE]).astype(BF16)
        k_ref[:, h * HEAD_PAD + QK_NOPE:(h + 1) * HEAD_PAD] = (kpe_w * r).astype(BF16)
    v_ref[...] = kv[:, hv:].astype(BF16)


def _proj_body(d, cc, n_s_tiles, nx, *refs):
    x_refs = refs[:nx]
    (mod_ref, n1w_ref, win_ref, qanw_ref, wqb_ref, kvanw_ref, wkvb_ref, qnw_ref, knw_ref, rc_ref, rs_ref,
     glu_ref, q_ref, k_ref, v_ref, ckv_ref, kpe_ref) = refs[nx:]
    m = mod_ref[0]
    x = _x_load(x_refs, pl.program_id(0) < n_s_tiles)
    h = _rms(x, n1w_ref[...]) * (1.0 + m[:, d:2 * d]) + m[:, 0:d]
    proj = jnp.dot(h.astype(BF16), win_ref[...], preferred_element_type=F32)
    o1 = 2 * cc
    o2 = o1 + Q_LORA
    o3 = o2 + KV_LORA
    glu_ref[...] = proj[:, :cc] * jax.nn.sigmoid(proj[:, cc:o1])

    rc = rc_ref[...]
    rs = rs_ref[...]
    qa = _rms(proj[:, o1:o2], qanw_ref[...]).astype(BF16)
    qb = jnp.dot(qa, wqb_ref[...], preferred_element_type=F32)
    qnw = qnw_ref[...]
    for hd in range(MLA_HEADS):
        qh = qb[:, hd * HEAD_PAD:(hd + 1) * HEAD_PAD]
        ss = jnp.sum(qh * qh, axis=-1, keepdims=True)
        r = lax.rsqrt(ss * (1.0 / QK_HEAD) + EPS)
        qn = qh * r * qnw
        q_ref[:, hd * HEAD_PAD:hd * HEAD_PAD + QK_NOPE] = qn[:, :QK_NOPE].astype(BF16)
        q_ref[:, hd * HEAD_PAD + QK_NOPE:(hd + 1) * HEAD_PAD] = _rope128(qn[:, QK_NOPE:], rc, rs).astype(BF16)

    ckv = _rms(proj[:, o2:o3], kvanw_ref[...])
    ckv_ref[...] = ckv
    kpe = proj[:, o3:]
    kpe_ref[...] = kpe
    kv = jnp.dot(ckv.astype(BF16), wkvb_ref[...], preferred_element_type=F32)
    _assemble_kv(kv, kpe, knw_ref[...], (rc, rs), k_ref, v_ref)


def _proj(g, layer, x, mod, n1w, win, qanw, wqb, kvanw, wkvb, qnw, knw, rope_c, rope_s, tm):
    d = g.d
    cc = d // 2
    nin = win.shape[1]
    nq = MLA_HEADS * HEAD_PAD
    nv = MLA_HEADS * V_HEAD
    s_tiles = g.dec_s // tm

    def rope_idx(i):
        return (jnp.where(i * tm < g.ns, i % s_tiles, s_tiles), 0)

    row = lambda n: pl.BlockSpec((tm, n), lambda i: (i, 0))
    mrow = g.mod_row(layer, tm)
    x_specs, xs = _x_in(g, tm, x)
    return pl.pallas_call(
        functools.partial(_proj_body, d, cc, g.ns // tm, len(xs)),
        grid=(g.t // tm,),
        in_specs=x_specs + [
            pl.BlockSpec((1, 1, N_MOD * d), lambda i: (mrow(i), 0, 0)),
            _const_spec((1, d)),
            _const_spec((d, nin)),
            _const_spec((1, Q_LORA)),
            _const_spec((Q_LORA, nq)),
            _const_spec((1, KV_LORA)),
            _const_spec((KV_LORA, nq)),
            _const_spec((1, HEAD_PAD)),
            _const_spec((1, HEAD_PAD)),
            pl.BlockSpec((tm, LANES), rope_idx),
            pl.BlockSpec((tm, LANES), rope_idx),
        ],
        out_specs=[row(cc), row(nq), row(nq), row(nv), row(KV_LORA), row(LANES)],
        out_shape=[
            jax.ShapeDtypeStruct((g.t, cc), F32),
            jax.ShapeDtypeStruct((g.t, nq), BF16),
            jax.ShapeDtypeStruct((g.t, nq), BF16),
            jax.ShapeDtypeStruct((g.t, nv), BF16),
            jax.ShapeDtypeStruct((g.t, KV_LORA), F32),
            jax.ShapeDtypeStruct((g.t, LANES), F32),
        ],
        compiler_params=_cparams(("arbitrary",)),
        name="even_proj",
    )(*xs, mod, n1w, win, qanw, wqb, kvanw, wkvb, qnw, knw, rope_c, rope_s)


def _ctx_kv_body(ckv_ref, kpe_ref, wkvb_ref, knw_ref, k_ref, v_ref):
    kv = jnp.dot(ckv_ref[...], wkvb_ref[...], preferred_element_type=F32)
    _assemble_kv(kv, kpe_ref[...], knw_ref[...], None, k_ref, v_ref)


def _ctx_kv(ckv, kpe, wkvb, knw, tm):
    n = ckv.shape[0]
    nq = MLA_HEADS * HEAD_PAD
    nv = MLA_HEADS * V_HEAD
    row = lambda w: pl.BlockSpec((tm, w), lambda i: (i, 0))
    return pl.pallas_call(
        _ctx_kv_body,
        grid=(n // tm,),
        in_specs=[row(KV_LORA), row(LANES), _const_spec((KV_LORA, nq)), _const_spec((1, HEAD_PAD))],
        out_specs=[row(nq), row(nv)],
        out_shape=[jax.ShapeDtypeStruct((n, nq), BF16), jax.ShapeDtypeStruct((n, nv), BF16)],
        compiler_params=_cparams(("arbitrary",)),
        name="ctx_kv",
    )(ckv, kpe, wkvb, knw)


_NT = (((1,), (1,)), ((), ()))
_EXP2_SCALE = math.log2(math.e) * QK_HEAD ** -0.5


ATTN_SUB = 256
ATTN_KCHUNK = 512


def _attn_sample_body(q_ref, kc_ref, kl_ref, vc_ref, vl_ref, o_ref, s_ref):
    past = kc_ref.shape[0]
    nsub = q_ref.shape[0] // ATTN_SUB
    for i in range(nsub):
        q = q_ref[i * ATTN_SUB:(i + 1) * ATTN_SUB, :]
        s_ref[i, :, :past] = lax.dot_general(q, kc_ref[...], _NT, preferred_element_type=F32)
        s_ref[i, :, past:] = lax.dot_general(q, kl_ref[...], _NT, preferred_element_type=F32)
    nk = kl_ref.shape[0]
    for i in range(nsub):
        m = jnp.max(s_ref[i], axis=-1, keepdims=True)
        p = jnp.exp2((s_ref[i, :, :past] - m) * _EXP2_SCALE)
        l = jnp.sum(p, axis=-1, keepdims=True)
        o = jnp.dot(p.astype(BF16), vc_ref[...], preferred_element_type=F32)
        for c0 in range(0, nk, ATTN_KCHUNK):
            p = jnp.exp2((s_ref[i, :, past + c0:past + c0 + ATTN_KCHUNK] - m) * _EXP2_SCALE)
            l = l + jnp.sum(p, axis=-1, keepdims=True)
            o = o + jnp.dot(p.astype(BF16), vl_ref[c0:c0 + ATTN_KCHUNK, :], preferred_element_type=F32)
        o_ref[i * ATTN_SUB:(i + 1) * ATTN_SUB, :] = (o / l).astype(BF16)


def _attn_sample(g, q, k, v, kc, vc, past, tq):
    nqt = g.dec_s // tq
    return pl.pallas_call(
        _attn_sample_body,
        grid=(g.dec_b, MLA_HEADS, nqt),
        in_specs=[
            pl.BlockSpec((tq, HEAD_PAD), lambda b, h, i: (b * nqt + i, h)),
            pl.BlockSpec((past, HEAD_PAD), lambda b, h, i: (b, h)),
            pl.BlockSpec((g.dec_s, HEAD_PAD), lambda b, h, i: (b, h)),
            pl.BlockSpec((past, V_HEAD), lambda b, h, i: (b, h)),
            pl.BlockSpec((g.dec_s, V_HEAD), lambda b, h, i: (b, h)),
        ],
        out_specs=pl.BlockSpec((tq, V_HEAD), lambda b, h, i: (b * nqt + i, h)),
        out_shape=jax.ShapeDtypeStruct((g.ns, MLA_HEADS * V_HEAD), BF16),
        scratch_shapes=[pltpu.VMEM((tq // ATTN_SUB, ATTN_SUB, past + g.dec_s), F32)],
        compiler_params=_cparams(("arbitrary", "arbitrary", "arbitrary")),
        name="attn_sample",
    )(q, kc, k, vc, v)


def _attn_prompt_body(q_ref, k_ref, v_ref, o_ref):
    s = lax.dot_general(q_ref[...], k_ref[...], _NT, preferred_element_type=F32)
    m = jnp.max(s, axis=-1, keepdims=True)
    p = jnp.exp2((s - m) * _EXP2_SCALE)
    l = jnp.sum(p, axis=-1, keepdims=True)
    o = jnp.dot(p.astype(BF16), v_ref[...], preferred_element_type=F32)
    o_ref[...] = (o / l).astype(BF16)


def _attn_prompt(g, q, k, v):
    off = g.ns // g.s
    return pl.pallas_call(
        _attn_prompt_body,
        grid=(g.b, MLA_HEADS),
        in_specs=[
            pl.BlockSpec((g.s, HEAD_PAD), lambda b, h: (off + b, h)),
            pl.BlockSpec((g.s, HEAD_PAD), lambda b, h: (off + b, h)),
            pl.BlockSpec((g.s, V_HEAD), lambda b, h: (off + b, h)),
        ],
        out_specs=pl.BlockSpec((g.s, V_HEAD), lambda b, h: (b, h)),
        out_shape=jax.ShapeDtypeStruct((g.np_, MLA_HEADS * V_HEAD), BF16),
        compiler_params=_cparams(("arbitrary", "arbitrary")),
        name="attn_prompt",
    )(q, k, v)


def _chunk_pos(g, ch):
    n = pl.program_id(0)
    n_s = g.ns // ch
    cps_s = g.dec_s // ch
    cps_p = g.s // ch
    is_s = n < n_s
    j = jnp.where(is_s, n % cps_s, (n - n_s) % cps_p)
    cps = jnp.where(is_s, cps_s, cps_p)
    return j, cps


def _halo_specs(g, ch, halo, width):
    per = ch // halo
    last = g.t // halo - 1
    main = pl.BlockSpec((ch, width), lambda n: (n, 0))
    left = pl.BlockSpec((halo, width), lambda n: (jnp.maximum(n * per - 1, 0), 0))
    right = pl.BlockSpec((halo, width), lambda n: (jnp.minimum((n + 1) * per, last), 0))
    return main, left, right


CONV_ROWS = 32


def _conv_body(g, ch, x_ref, xl_ref, xr_ref, w_ref, b_ref, lnw_ref, lnb_ref, o_ref, xp_ref, xs_ref):
    j, cps = _chunk_pos(g, ch)
    xp_ref[0:CONV_HALO] = jnp.where(j > 0, xl_ref[...], 0.0)
    xp_ref[CONV_HALO:CONV_HALO + ch] = x_ref[...]
    xp_ref[CONV_HALO + ch:2 * CONV_HALO + ch] = jnp.where(j < cps - 1, xr_ref[...], 0.0)
    bias = b_ref[...]
    lnw = lnw_ref[...]
    lnb = lnb_ref[...]
    base = CONV_HALO - CONV_K // 2

    span = xs_ref.shape[1]
    for ph in range(1, SUBLANES):
        xs_ref[ph] = xp_ref[ph:ph + span, :]

    def src(lo, ph):
        return xp_ref[lo:lo + SUBLANES, :] if ph == 0 else xs_ref[ph, lo:lo + SUBLANES, :]

    for r0 in range(0, ch, CONV_ROWS):
        accs = [None] * (CONV_ROWS // SUBLANES)
        for k in range(CONV_K):
            wk = w_ref[k]
            ph = (base + k) % SUBLANES
            for jj in range(len(accs)):
                term = src(r0 + jj * SUBLANES + base + k - ph, ph) * wk
                accs[jj] = term if accs[jj] is None else accs[jj] + term
        acc = jnp.concatenate(accs, axis=0) + bias
        mu = jnp.mean(acc, axis=-1, keepdims=True)
        xc = acc - mu
        var = jnp.mean(xc * xc, axis=-1, keepdims=True)
        y = xc * lax.rsqrt(var + EPS) * lnw + lnb
        o_ref[r0:r0 + CONV_ROWS, :] = _silu(y).astype(BF16)


def _conv(g, glu, w, b, lnw, lnb, ch):
    cc = glu.shape[1]
    main, left, right = _halo_specs(g, ch, CONV_HALO, cc)
    return pl.pallas_call(
        functools.partial(_conv_body, g, ch),
        grid=(g.t // ch,),
        in_specs=[main, left, right, _const_spec((CONV_K, SUBLANES, cc)), _const_spec((1, cc)),
                  _const_spec((1, cc)), _const_spec((1, cc))],
        out_specs=pl.BlockSpec((ch, cc), lambda n: (n, 0)),
        out_shape=jax.ShapeDtypeStruct((g.t, cc), BF16),
        scratch_shapes=[pltpu.VMEM((ch + 2 * CONV_HALO, cc), F32),
                        pltpu.VMEM((SUBLANES, ch + 2 * CONV_HALO - SUBLANES, cc), F32)],
        compiler_params=_cparams(("arbitrary",)),
        name="conv_module",
    )(glu, glu, glu, w, b, lnw, lnb)


def _mix_out_body(d, cc, n_s_tiles, nx, *refs):
    x_refs = refs[:nx]
    mod_ref, ca_ref, as_ref, ap_ref, wo_ref, o_ref = refs[nx:]
    i = pl.program_id(0)
    attn = jnp.where(i < n_s_tiles, as_ref[...], ap_ref[...])
    y = jnp.dot(ca_ref[...], wo_ref[0:cc, :], preferred_element_type=F32)
    y = y + jnp.dot(attn, wo_ref[cc:, :], preferred_element_type=F32)
    g1 = mod_ref[0][:, 2 * d:3 * d]
    o_ref[...] = _x_load(x_refs, i < n_s_tiles) + g1 * y


def _mix_out(g, layer, x, mod, cact, attn_s, attn_p, wo, tm):
    d = g.d
    cc = cact.shape[1]
    na = attn_s.shape[1]
    n_s_tiles = g.ns // tm
    mrow = g.mod_row(layer, tm)
    x_specs, xs = _x_in(g, tm, x)
    return pl.pallas_call(
        functools.partial(_mix_out_body, d, cc, n_s_tiles, len(xs)),
        grid=(g.t // tm,),
        in_specs=x_specs + [
            pl.BlockSpec((1, 1, N_MOD * d), lambda i: (mrow(i), 0, 0)),
            pl.BlockSpec((tm, cc), lambda i: (i, 0)),
            pl.BlockSpec((tm, na), lambda i: (jnp.minimum(i, n_s_tiles - 1), 0)),
            pl.BlockSpec((tm, na), lambda i: (jnp.maximum(i - n_s_tiles, 0), 0)),
            _const_spec((cc + na, d)),
        ],
        out_specs=pl.BlockSpec((tm, d), lambda i: (i, 0)),
        out_shape=jax.ShapeDtypeStruct((g.t, d), F32),
        compiler_params=_cparams(("arbitrary",)),
        name="even_out",
    )(*xs, mod, cact, attn_s, attn_p, wo)


POOL_ROWS = 32


def _pool_body(g, ch, x_ref, xl_ref, xr_ref, mod_ref, n1w_ref, pw_ref, ps_ref, o_ref, hb_ref, mx_ref):
    d = g.d
    pg = d // len(POOL_WINDOWS)
    j, cps = _chunk_pos(g, ch)
    m = mod_ref[0]
    n1w = n1w_ref[...]
    scale = 1.0 + m[:, d:2 * d]
    shift = m[:, 0:d]

    def norm(x):
        return _rms(x, n1w) * scale + shift

    hb_ref[0:POOL_HALO] = jnp.where(j > 0, norm(xl_ref[...]), 0.0)
    hb_ref[POOL_HALO:POOL_HALO + ch] = norm(x_ref[...])
    hb_ref[POOL_HALO + ch:2 * POOL_HALO + ch] = jnp.where(j < cps - 1, norm(xr_ref[...]), 0.0)
    seq_len = cps * ch

    for r0 in range(0, ch, POOL_ROWS):
        pos = j * ch + r0 + lax.broadcasted_iota(jnp.int32, (POOL_ROWS, 1), 0)
        for gi, w in enumerate(POOL_WINDOWS):
            cols = slice(gi * pg, (gi + 1) * pg)
            lo = r0 + POOL_HALO - w // 2
            acc = hb_ref[lo:lo + POOL_ROWS, cols]
            for dd in range(1, w):
                acc = acc + hb_ref[lo + dd:lo + dd + POOL_ROWS, cols]
            cnt = jnp.minimum(pos + w // 2, seq_len) - jnp.maximum(pos - w // 2, 0)
            ctr = hb_ref[r0 + POOL_HALO:r0 + POOL_HALO + POOL_ROWS, cols]
            mx_ref[r0:r0 + POOL_ROWS, cols] = (acc / cnt.astype(F32) - ctr).astype(BF16)

    g1 = m[:, 2 * d:3 * d]
    ps = ps_ref[...]
    for gi in range(len(POOL_WINDOWS)):
        cols = slice(gi * pg, (gi + 1) * pg)
        y = jnp.dot(mx_ref[:, cols], pw_ref[gi], preferred_element_type=F32)
        o_ref[:, cols] = x_ref[:, cols] + g1[:, cols] * (y * ps[:, cols])


def _pool(g, layer, x, mod, n1w, pw, ps, ch):
    d = g.d
    pg = d // len(POOL_WINDOWS)
    main, left, right = _halo_specs(g, ch, POOL_HALO, d)
    mrow = g.mod_row(layer, ch)
    return pl.pallas_call(
        functools.partial(_pool_body, g, ch),
        grid=(g.t // ch,),
        in_specs=[main, left, right,
                  pl.BlockSpec((1, 1, N_MOD * d), lambda n: (mrow(n), 0, 0)),
                  _const_spec((1, d)),
                  _const_spec((len(POOL_WINDOWS), pg, pg)),
                  _const_spec((1, d))],
        out_specs=pl.BlockSpec((ch, d), lambda n: (n, 0)),
        out_shape=jax.ShapeDtypeStruct((g.t, d), F32),
        scratch_shapes=[pltpu.VMEM((ch + 2 * POOL_HALO, d), F32), pltpu.VMEM((ch, d), BF16)],
        compiler_params=_cparams(("arbitrary",)),
        name="pool_mixer",
    )(x, x, x, mod, n1w, pw, ps)


def _ffn_body(d, nj, n_s_tiles, split_out, x_ref, mod_ref, n2w_ref, wg_ref, wu_ref, wd_ref, *refs):
    if split_out:
        os_ref, op_ref, h_ref, acc_ref = refs
    else:
        acc_ref, h_ref = refs
    i = pl.program_id(0)
    j = pl.program_id(1)

    @pl.when(j == 0)
    def _():
        m = mod_ref[0]
        h = _rms(x_ref[...], n2w_ref[...]) * (1.0 + m[:, 4 * d:5 * d]) + m[:, 3 * d:4 * d]
        h_ref[...] = h.astype(BF16)
        acc_ref[...] = jnp.zeros_like(acc_ref)

    h = h_ref[...]
    gate = jnp.dot(h, wg_ref[...], preferred_element_type=F32)
    up = jnp.dot(h, wu_ref[...], preferred_element_type=F32)
    a = (_silu(gate) * up).astype(BF16)
    acc_ref[...] += jnp.dot(a, wd_ref[...], preferred_element_type=F32)

    def result():
        return x_ref[...] + mod_ref[0][:, 5 * d:6 * d] * acc_ref[...]

    if split_out:
        @pl.when((j == nj - 1) & (i < n_s_tiles))
        def _():
            os_ref[...] = result()

        @pl.when((j == nj - 1) & (i >= n_s_tiles))
        def _():
            op_ref[...] = result()
    else:
        @pl.when(j == nj - 1)
        def _():
            acc_ref[...] = result()


def _ffn(g, layer, x, mod, n2w, wg, wu, wd, tm, tf, split_out=False):
    d = g.d
    dff = wg.shape[2]
    nj = dff // tf
    n_s_tiles = g.ns // tm
    mrow = g.mod_row(layer, tm)
    if split_out:
        out_specs = [pl.BlockSpec((tm, d), lambda i, j: (jnp.minimum(i, n_s_tiles - 1), 0)),
                     pl.BlockSpec((tm, d), lambda i, j: (jnp.maximum(i - n_s_tiles, 0), 0))]
        out_shape = [jax.ShapeDtypeStruct((g.ns, d), F32), jax.ShapeDtypeStruct((g.np_, d), F32)]
        scratch = [pltpu.VMEM((tm, d), BF16), pltpu.VMEM((tm, d), F32)]
    else:
        out_specs = pl.BlockSpec((tm, d), lambda i, j: (i, 0))
        out_shape = jax.ShapeDtypeStruct((g.t, d), F32)
        scratch = [pltpu.VMEM((tm, d), BF16)]
    return pl.pallas_call(
        functools.partial(_ffn_body, d, nj, n_s_tiles, split_out),
        grid=(g.t // tm, nj),
        in_specs=[
            pl.BlockSpec((tm, d), lambda i, j: (i, 0)),
            pl.BlockSpec((1, 1, N_MOD * d), lambda i, j: (mrow(i), 0, 0)),
            pl.BlockSpec((1, d), lambda i, j: (0, 0)),
            pl.BlockSpec((None, d, tf), lambda i, j: (layer, 0, j)),
            pl.BlockSpec((None, d, tf), lambda i, j: (layer, 0, j)),
            pl.BlockSpec((None, tf, d), lambda i, j: (layer, j, 0)),
        ],
        out_specs=out_specs,
        out_shape=out_shape,
        scratch_shapes=scratch,
        compiler_params=_cparams(("arbitrary", "arbitrary")),
        name="ffn",
    )(x, mod, n2w, wg, wu, wd)


def _rope_tables(g, tm):
    rows = g.dec_s // GRID_W
    row = jnp.broadcast_to(jnp.arange(rows, dtype=F32)[:, None], (rows, GRID_W)).reshape(-1)
    col = jnp.broadcast_to(jnp.arange(GRID_W, dtype=F32)[None, :], (rows, GRID_W)).reshape(-1)
    n_freq = QK_ROPE // 4
    inv_freq = 1.0 / (ROPE_THETA ** (jnp.arange(n_freq, dtype=F32) / n_freq))
    ang = jnp.concatenate([row[:, None] * inv_freq, col[:, None] * inv_freq], axis=-1)
    cos, sin = jnp.cos(ang), jnp.sin(ang)
    z = jnp.zeros_like(cos)
    c = jnp.concatenate([cos, z, cos, z], axis=-1)
    s = jnp.concatenate([-sin, z, sin, z], axis=-1)
    c = jnp.concatenate([c, jnp.ones((tm, LANES), F32)], axis=0)
    s = jnp.concatenate([s, jnp.zeros((tm, LANES), F32)], axis=0)
    return c, s


def kernel(x_prompt, x_sample, cache_ckv, cache_kpe, c, c_ctx, norm1_w, norm2_w, w_mod, b_mod, w_in, conv_dw_w, conv_dw_b, conv_ln_w, conv_ln_b, q_a_norm_w, w_q_b, kv_a_norm_w, w_kv_b, q_norm_w, k_norm_w, w_out, pool_w, pool_scale, ffn_w_gate, ffn_w_up, ffn_w_down):
    b, s, d = x_prompt.shape
    dec_b, dec_s, _ = x_sample.shape
    past = cache_ckv.shape[2]
    depth = norm1_w.shape[0]
    g = _Geom(dec_b, dec_s, b, s, d)
    cc = d // 2
    assert dec_b + 1 <= MOD_ROWS and g.ns % s == 0 and dec_s % GRID_W == 0

    tm_proj = min(256, s)
    tm_out = min(512, g.np_, dec_s)
    tm_ffn = min(1024, g.np_, dec_s)
    tm_ffn_last = min(512, g.np_, dec_s)
    tf = 512
    ch = s
    tq = min(2 * ATTN_SUB, dec_s)

    cond = jnp.concatenate([c, c_ctx[None, :], jnp.zeros((MOD_ROWS - dec_b - 1, d), F32)], axis=0)
    mod = _modulation(cond, w_mod, b_mod).reshape(depth * MOD_ROWS, 1, N_MOD * d)

    rope_c, rope_s = _rope_tables(g, tm_proj)
    x = (x_sample.reshape(g.ns, d), x_prompt.reshape(g.np_, d))
    wg_all = ffn_w_gate.astype(BF16)
    wu_all = ffn_w_up.astype(BF16)
    wd_all = ffn_w_down.astype(BF16)

    new_ckv, new_kpe = [], []
    for layer in range(depth):
        n1w = norm1_w[layer][None, :]
        if layer % 2 == 0:
            e = layer // 2
            win = jnp.concatenate([w_in[e][:, :2 * cc + Q_LORA + KV_LORA],
                                   _pad_rope(w_in[e][:, 2 * cc + Q_LORA + KV_LORA:])], axis=-1).astype(BF16)
            wq = w_q_b[e].reshape(Q_LORA, MLA_HEADS, QK_HEAD)
            wqb = jnp.concatenate([wq[..., :QK_NOPE], _pad_rope(wq[..., QK_NOPE:])], axis=-1)
            wqb = wqb.reshape(Q_LORA, MLA_HEADS * HEAD_PAD).astype(BF16)
            wkv = w_kv_b[e].reshape(KV_LORA, MLA_HEADS, QK_NOPE + V_HEAD)
            wkvb = jnp.concatenate([wkv[..., :QK_NOPE].reshape(KV_LORA, -1),
                                    wkv[..., QK_NOPE:].reshape(KV_LORA, -1)], axis=-1).astype(BF16)
            qnw = jnp.concatenate([q_norm_w[e][:QK_NOPE], _pad_rope(q_norm_w[e][QK_NOPE:])])[None, :]
            knw = jnp.concatenate([k_norm_w[e][:QK_NOPE], _pad_rope(k_norm_w[e][QK_NOPE:])])[None, :]

            glu, q, k, v, ckv, kpe = _proj(
                g, layer, x, mod, n1w, win, q_a_norm_w[e][None, :], wqb, kv_a_norm_w[e][None, :], wkvb,
                qnw, knw, rope_c, rope_s, tm_proj)
            new_ckv.append(ckv[g.ns:].reshape(b, s, KV_LORA))
            new_kpe.append(_unpad_rope(kpe[g.ns:]).reshape(b, s, QK_ROPE))

            kc, vc = _ctx_kv(cache_ckv[:, e].reshape(dec_b * past, KV_LORA).astype(BF16),
                             _pad_rope(cache_kpe[:, e]).reshape(dec_b * past, LANES), wkvb, knw, past)
            attn_s = _attn_sample(g, q, k, v, kc, vc, past, tq)
            attn_p = _attn_prompt(g, q, k, v)
            w_taps = jnp.broadcast_to(conv_dw_w[e][:, None, :], (CONV_K, SUBLANES, cc))
            cact = _conv(g, glu, w_taps, conv_dw_b[e][None, :], conv_ln_w[e][None, :],
                         conv_ln_b[e][None, :], ch)
            x = _mix_out(g, layer, x, mod, cact, attn_s, attn_p, w_out[e].astype(BF16), tm_out)
        else:
            o = layer // 2
            x = _pool(g, layer, x, mod, n1w, pool_w[o].astype(BF16), pool_scale[o][None, :], ch)
        last = layer == depth - 1
        x = _ffn(g, layer, x, mod, norm2_w[layer][None, :], wg_all, wu_all, wd_all,
                 tm_ffn_last if last else tm_ffn, tf, split_out=last)

    y_sample = x[0].reshape(dec_b, dec_s, d)
    y_prompt = x[1].reshape(b, s, d)
    return (y_prompt, y_sample, jnp.stack(new_ckv, axis=1), jnp.stack(new_kpe, axis=1))
```

```python
import functools
import math

import jax
import jax.numpy as jnp
from jax import lax
from jax.experimental import pallas as pl
from jax.experimental.pallas import tpu as pltpu

F32 = jnp.float32
BF16 = jnp.bfloat16

EPS = 1e-6
GRID_W = 64
CONV_K = 31
MLA_HEADS = 8
QK_NOPE = 128
QK_ROPE = 64
V_HEAD = 128
QK_HEAD = QK_NOPE + QK_ROPE
Q_LORA = 768
KV_LORA = 512
ROPE_THETA = 10000.0
POOL_WINDOWS = (2, 4, 8, 16)
N_MOD = 6

LANES = 128
SUBLANES = 8
HEAD_PAD = 2 * LANES
V_PAD = 2 * LANES
ROPE_HALF = QK_ROPE // 2
MOD_ROWS = 16
CONV_HALO = 16
POOL_HALO = 8
VMEM_LIMIT = 62 * 1024 * 1024


def _cparams(sem, vmem=VMEM_LIMIT):
    return pltpu.CompilerParams(dimension_semantics=sem, vmem_limit_bytes=vmem)


def _const_spec(shape):
    nd = len(shape)
    return pl.BlockSpec(shape, lambda *_: (0,) * nd, pipeline_mode=pl.Buffered(1))


def _rms(x, w):
    ms = jnp.mean(x * x, axis=-1, keepdims=True)
    return x * lax.rsqrt(ms + EPS) * w


def _silu(x):
    return x * jax.nn.sigmoid(x)


def _pad_rope(w):
    z = jnp.zeros(w.shape[:-1] + (ROPE_HALF,), w.dtype)
    return jnp.concatenate([w[..., 0::2], z, w[..., 1::2], z], axis=-1)


def _unpad_rope(w):
    return jnp.stack([w[..., 0:ROPE_HALF], w[..., 2 * ROPE_HALF:3 * ROPE_HALF]], axis=-1).reshape(
        w.shape[:-1] + (QK_ROPE,))


class _Geom:
    def __init__(self, dec_b, dec_s, b, s, d):
        self.dec_b, self.dec_s, self.b, self.s, self.d = dec_b, dec_s, b, s, d
        self.ns = dec_b * dec_s
        self.np_ = b * s
        self.t = self.ns + self.np_

    def mod_row(self, layer, tile):
        def f(i):
            tok = i * tile
            return layer * MOD_ROWS + jnp.where(tok < self.ns, tok // self.dec_s, self.dec_b)
        return f


def _x_in(g, tm, x, idx=lambda i: i):
    d = g.d
    if not isinstance(x, tuple):
        return [pl.BlockSpec((tm, d), lambda *a: (idx(*a), 0))], (x,)
    n_s = g.ns // tm
    return [pl.BlockSpec((tm, d), lambda *a: (jnp.minimum(idx(*a), n_s - 1), 0)),
            pl.BlockSpec((tm, d), lambda *a: (jnp.maximum(idx(*a) - n_s, 0), 0))], x


def _x_load(x_refs, is_sample):
    if len(x_refs) == 1:
        return x_refs[0][...]
    return jnp.where(is_sample, x_refs[0][...], x_refs[1][...])


def _mod_body(c_ref, w_ref, b_ref, o_ref):
    s = _silu(c_ref[...]).astype(BF16)
    w = w_ref[0].astype(BF16)
    o_ref[0] = jnp.dot(s, w, preferred_element_type=F32) + b_ref[0]


def _modulation(cond, w_mod, b_mod):
    depth, d, n = w_mod.shape
    tn = 1024
    return pl.pallas_call(
        _mod_body,
        grid=(depth, n // tn),
        in_specs=[
            pl.BlockSpec((MOD_ROWS, d), lambda l, j: (0, 0)),
            pl.BlockSpec((1, d, tn), lambda l, j: (l, 0, j)),
            pl.BlockSpec((1, 1, tn), lambda l, j: (l, 0, j)),
        ],
        out_specs=pl.BlockSpec((1, MOD_ROWS, tn), lambda l, j: (l, 0, j)),
        out_shape=jax.ShapeDtypeStruct((depth, MOD_ROWS, n), F32),
        compiler_params=_cparams(("arbitrary", "arbitrary")),
        name="modulation",
    )(cond, w_mod, b_mod.reshape(depth, 1, n))


def _rope128(x, c, s):
    return x * c + pltpu.roll(x, 2 * ROPE_HALF, 1) * s


def _assemble_kv(kv, kpe, knw, rope, k_ref, v_ref):
    hv = MLA_HEADS * QK_NOPE
    kpe_ss = jnp.sum(kpe * kpe, axis=-1, keepdims=True)
    kpe_w = kpe * knw[:, QK_NOPE:]
    if rope is not None:
        kpe_w = _rope128(kpe_w, rope[0], rope[1])
    for h in range(MLA_HEADS):
        kn = kv[:, h * QK_NOPE:(h + 1) * QK_NOPE]
        ss = jnp.sum(kn * kn, axis=-1, keepdims=True) + kpe_ss
        r = lax.rsqrt(ss * (1.0 / QK_HEAD) + EPS)
        k_ref[:, h * HEAD_PAD:h * HEAD_PAD + QK_NOPE] = (kn * r * knw[:, :QK_NOPE]).astype(BF16)
        k_ref[:, h * HEAD_PAD + QK_NOPE:(h + 1) * HEAD_PAD] = (kpe_w * r).astype(BF16)
        v_ref[:, h * V_PAD:h * V_PAD + V_HEAD] = kv[:, hv + h * V_HEAD:hv + (h + 1) * V_HEAD].astype(BF16)
        v_ref[:, h * V_PAD + V_HEAD:(h + 1) * V_PAD] = jnp.ones((kv.shape[0], V_PAD - V_HEAD), BF16)


def _proj_body(d, cc, n_s_tiles, nx, *refs):
    x_refs = refs[:nx]
    (mod_ref, n1w_ref, win_ref, qanw_ref, wqb_ref, kvanw_ref, wkvb_ref, qnw_ref, knw_ref, rc_ref, rs_ref,
     glu_ref, q_ref, k_ref, v_ref, ckv_ref, kpe_ref) = refs[nx:]
    m = mod_ref[0]
    x = _x_load(x_refs, pl.program_id(0) < n_s_tiles)
    h = _rms(x, n1w_ref[...]) * (1.0 + m[:, d:2 * d]) + m[:, 0:d]
    proj = jnp.dot(h.astype(BF16), win_ref[...], preferred_element_type=F32)
    o1 = 2 * cc
    o2 = o1 + Q_LORA
    o3 = o2 + KV_LORA
    glu_ref[...] = proj[:, :cc] * jax.nn.sigmoid(proj[:, cc:o1])

    rc = rc_ref[...]
    rs = rs_ref[...]
    qa = _rms(proj[:, o1:o2], qanw_ref[...]).astype(BF16)
    qb = jnp.dot(qa, wqb_ref[...], preferred_element_type=F32)
    qnw = qnw_ref[...]
    for hd in range(MLA_HEADS):
        qh = qb[:, hd * HEAD_PAD:(hd + 1) * HEAD_PAD]
        ss = jnp.sum(qh * qh, axis=-1, keepdims=True)
        r = lax.rsqrt(ss * (1.0 / QK_HEAD) + EPS)
        qn = qh * r * qnw
        q_ref[:, hd * HEAD_PAD:hd * HEAD_PAD + QK_NOPE] = qn[:, :QK_NOPE].astype(BF16)
        q_ref[:, hd * HEAD_PAD + QK_NOPE:(hd + 1) * HEAD_PAD] = _rope128(qn[:, QK_NOPE:], rc, rs).astype(BF16)

    ckv = _rms(proj[:, o2:o3], kvanw_ref[...])
    ckv_ref[...] = ckv
    kpe = proj[:, o3:]
    kpe_ref[...] = kpe
    kv = jnp.dot(ckv.astype(BF16), wkvb_ref[...], preferred_element_type=F32)
    _assemble_kv(kv, kpe, knw_ref[...], (rc, rs), k_ref, v_ref)


def _proj(g, layer, x, mod, n1w, win, qanw, wqb, kvanw, wkvb, qnw, knw, rope_c, rope_s, tm):
    d = g.d
    cc = d // 2
    nin = win.shape[1]
    nq = MLA_HEADS * HEAD_PAD
    nv = MLA_HEADS * V_PAD
    s_tiles = g.dec_s // tm

    def rope_idx(i):
        return (jnp.where(i * tm < g.ns, i % s_tiles, s_tiles), 0)

    row = lambda n: pl.BlockSpec((tm, n), lambda i: (i, 0))
    mrow = g.mod_row(layer, tm)
    x_specs, xs = _x_in(g, tm, x)
    return pl.pallas_call(
        functools.partial(_proj_body, d, cc, g.ns // tm, len(xs)),
        grid=(g.t // tm,),
        in_specs=x_specs + [
            pl.BlockSpec((1, 1, N_MOD * d), lambda i: (mrow(i), 0, 0)),
            _const_spec((1, d)),
            _const_spec((d, nin)),
            _const_spec((1, Q_LORA)),
            _const_spec((Q_LORA, nq)),
            _const_spec((1, KV_LORA)),
            _const_spec((KV_LORA, nq)),
            _const_spec((1, HEAD_PAD)),
            _const_spec((1, HEAD_PAD)),
            pl.BlockSpec((tm, LANES), rope_idx),
            pl.BlockSpec((tm, LANES), rope_idx),
        ],
        out_specs=[row(cc), row(nq), row(nq), row(nv), row(KV_LORA), row(LANES)],
        out_shape=[
            jax.ShapeDtypeStruct((g.t, cc), F32),
            jax.ShapeDtypeStruct((g.t, nq), BF16),
            jax.ShapeDtypeStruct((g.t, nq), BF16),
            jax.ShapeDtypeStruct((g.t, nv), BF16),
            jax.ShapeDtypeStruct((g.t, KV_LORA), F32),
            jax.ShapeDtypeStruct((g.t, LANES), F32),
        ],
        compiler_params=_cparams(("arbitrary",)),
        name="even_proj",
    )(*xs, mod, n1w, win, qanw, wqb, kvanw, wkvb, qnw, knw, rope_c, rope_s)


def _ctx_kv_body(ckv_ref, kpe_ref, wkvb_ref, knw_ref, k_ref, v_ref):
    kv = jnp.dot(ckv_ref[...], wkvb_ref[...], preferred_element_type=F32)
    _assemble_kv(kv, kpe_ref[...], knw_ref[...], None, k_ref, v_ref)


def _ctx_kv(ckv, kpe, wkvb, knw, tm):
    n = ckv.shape[0]
    nq = MLA_HEADS * HEAD_PAD
    nv = MLA_HEADS * V_PAD
    row = lambda w: pl.BlockSpec((tm, w), lambda i: (i, 0))
    return pl.pallas_call(
        _ctx_kv_body,
        grid=(n // tm,),
        in_specs=[row(KV_LORA), row(LANES), _const_spec((KV_LORA, nq)), _const_spec((1, HEAD_PAD))],
        out_specs=[row(nq), row(nv)],
        out_shape=[jax.ShapeDtypeStruct((n, nq), BF16), jax.ShapeDtypeStruct((n, nv), BF16)],
        compiler_params=_cparams(("arbitrary",)),
        name="ctx_kv",
    )(ckv, kpe, wkvb, knw)


_NT = (((1,), (1,)), ((), ()))
_EXP2_SCALE = math.log2(math.e) * QK_HEAD ** -0.5

ATTN_SUB = 512
ATTN_KCHUNK = 1024


def _normalise(o):
    return (o[:, :V_HEAD] / o[:, V_HEAD:]).astype(BF16)


def _attn_sample_body(q_ref, kc_ref, kl_ref, vc_ref, vl_ref, o_ref, s0_ref, s1_ref):
    past = kc_ref.shape[0]
    nk = kl_ref.shape[0]
    sub = s0_ref.shape[0]
    nsub = q_ref.shape[0] // sub

    def rows(i):
        return pl.ds(pl.multiple_of(i * sub, sub), sub)

    def scores(i, s_ref):
        q = q_ref[rows(i), :]
        s_ref[:, :past] = lax.dot_general(q, kc_ref[...], _NT, preferred_element_type=F32)
        s_ref[:, past:] = lax.dot_general(q, kl_ref[...], _NT, preferred_element_type=F32)

    def finish(i, s_ref):
        m = jnp.max(s_ref[...], axis=-1, keepdims=True)
        p = jnp.exp2(s_ref[:, :past] - m)
        o = jnp.dot(p.astype(BF16), vc_ref[...], preferred_element_type=F32)
        kchunk = min(ATTN_KCHUNK, nk)
        for c0 in range(0, nk, kchunk):
            p = jnp.exp2(s_ref[:, past + c0:past + c0 + kchunk] - m)
            o = o + jnp.dot(p.astype(BF16), vl_ref[c0:c0 + kchunk, :], preferred_element_type=F32)
        o_ref[rows(i), :] = _normalise(o)

    scores(0, s0_ref)

    def pair(t, carry):
        i = 2 * t
        scores(i + 1, s1_ref)
        finish(i, s0_ref)
        scores(i + 2, s0_ref)
        finish(i + 1, s1_ref)
        return carry

    lax.fori_loop(0, nsub // 2 - 1, pair, 0)
    scores(nsub - 1, s1_ref)
    finish(nsub - 2, s0_ref)
    finish(nsub - 1, s1_ref)


def _attn_sample(g, q, k, v, kc, vc, past):
    sub = min(ATTN_SUB, g.dec_s // 2)
    assert g.dec_s % (2 * sub) == 0 and g.dec_s % min(ATTN_KCHUNK, g.dec_s) == 0
    score_buf = pltpu.VMEM((sub, past + g.dec_s), F32)
    return pl.pallas_call(
        _attn_sample_body,
        grid=(g.dec_b, MLA_HEADS),
        in_specs=[
            pl.BlockSpec((g.dec_s, HEAD_PAD), lambda b, h: (b, h)),
            pl.BlockSpec((past, HEAD_PAD), lambda b, h: (b, h)),
            pl.BlockSpec((g.dec_s, HEAD_PAD), lambda b, h: (b, h)),
            pl.BlockSpec((past, V_PAD), lambda b, h: (b, h)),
            pl.BlockSpec((g.dec_s, V_PAD), lambda b, h: (b, h)),
        ],
        out_specs=pl.BlockSpec((g.dec_s, V_HEAD), lambda b, h: (b, h)),
        out_shape=jax.ShapeDtypeStruct((g.ns, MLA_HEADS * V_HEAD), BF16),
        scratch_shapes=[score_buf, score_buf],
        compiler_params=_cparams(("arbitrary", "arbitrary")),
        name="attn_sample",
    )(q, kc, k, vc, v)


def _attn_prompt_body(q_ref, k_ref, v_ref, o_ref):
    for h in range(MLA_HEADS):
        qk = slice(h * HEAD_PAD, (h + 1) * HEAD_PAD)
        s = lax.dot_general(q_ref[:, qk], k_ref[:, qk], _NT, preferred_element_type=F32)
        m = jnp.max(s, axis=-1, keepdims=True)
        p = jnp.exp2(s - m)
        o = jnp.dot(p.astype(BF16), v_ref[:, h * V_PAD:(h + 1) * V_PAD], preferred_element_type=F32)
        o_ref[:, h * V_HEAD:(h + 1) * V_HEAD] = _normalise(o)


def _attn_prompt(g, q, k, v):
    off = g.ns // g.s
    return pl.pallas_call(
        _attn_prompt_body,
        grid=(g.b,),
        in_specs=[
            pl.BlockSpec((g.s, MLA_HEADS * HEAD_PAD), lambda b: (off + b, 0)),
            pl.BlockSpec((g.s, MLA_HEADS * HEAD_PAD), lambda b: (off + b, 0)),
            pl.BlockSpec((g.s, MLA_HEADS * V_PAD), lambda b: (off + b, 0)),
        ],
        out_specs=pl.BlockSpec((g.s, MLA_HEADS * V_HEAD), lambda b: (b, 0)),
        out_shape=jax.ShapeDtypeStruct((g.np_, MLA_HEADS * V_HEAD), BF16),
        compiler_params=_cparams(("arbitrary",)),
        name="attn_prompt",
    )(q, k, v)


def _chunk_pos(g, ch):
    n = pl.program_id(0)
    n_s = g.ns // ch
    cps_s = g.dec_s // ch
    cps_p = g.s // ch
    is_s = n < n_s
    j = jnp.where(is_s, n % cps_s, (n - n_s) % cps_p)
    cps = jnp.where(is_s, cps_s, cps_p)
    return j, cps


def _halo_specs(g, ch, halo, width):
    per = ch // halo
    last = g.t // halo - 1
    main = pl.BlockSpec((ch, width), lambda n: (n, 0))
    left = pl.BlockSpec((halo, width), lambda n: (jnp.maximum(n * per - 1, 0), 0))
    right = pl.BlockSpec((halo, width), lambda n: (jnp.minimum((n + 1) * per, last), 0))
    return main, left, right


CONV_ROWS = 32


def _conv_body(g, ch, x_ref, xl_ref, xr_ref, w_ref, b_ref, lnw_ref, lnb_ref, o_ref, xp_ref, xs_ref):
    j, cps = _chunk_pos(g, ch)
    xp_ref[0:CONV_HALO] = jnp.where(j > 0, xl_ref[...], 0.0)
    xp_ref[CONV_HALO:CONV_HALO + ch] = x_ref[...]
    xp_ref[CONV_HALO + ch:2 * CONV_HALO + ch] = jnp.where(j < cps - 1, xr_ref[...], 0.0)
    bias = b_ref[...]
    lnw = lnw_ref[...]
    lnb = lnb_ref[...]
    base = CONV_HALO - CONV_K // 2

    span = xs_ref.shape[1]
    for ph in range(1, SUBLANES):
        xs_ref[ph] = xp_ref[ph:ph + span, :]

    def src(lo, ph):
        return xp_ref[lo:lo + SUBLANES, :] if ph == 0 else xs_ref[ph, lo:lo + SUBLANES, :]

    for r0 in range(0, ch, CONV_ROWS):
        accs = [None] * (CONV_ROWS // SUBLANES)
        for k in range(CONV_K):
            wk = w_ref[k]
            ph = (base + k) % SUBLANES
            for jj in range(len(accs)):
                term = src(r0 + jj * SUBLANES + base + k - ph, ph) * wk
                accs[jj] = term if accs[jj] is None else accs[jj] + term
        acc = jnp.concatenate(accs, axis=0) + bias
        mu = jnp.mean(acc, axis=-1, keepdims=True)
        xc = acc - mu
        var = jnp.mean(xc * xc, axis=-1, keepdims=True)
        y = xc * lax.rsqrt(var + EPS) * lnw + lnb
        o_ref[r0:r0 + CONV_ROWS, :] = _silu(y).astype(BF16)


def _conv(g, glu, w, b, lnw, lnb, ch):
    cc = glu.shape[1]
    main, left, right = _halo_specs(g, ch, CONV_HALO, cc)
    return pl.pallas_call(
        functools.partial(_conv_body, g, ch),
        grid=(g.t // ch,),
        in_specs=[main, left, right, _const_spec((CONV_K, SUBLANES, cc)), _const_spec((1, cc)),
                  _const_spec((1, cc)), _const_spec((1, cc))],
        out_specs=pl.BlockSpec((ch, cc), lambda n: (n, 0)),
        out_shape=jax.ShapeDtypeStruct((g.t, cc), BF16),
        scratch_shapes=[pltpu.VMEM((ch + 2 * CONV_HALO, cc), F32),
                        pltpu.VMEM((SUBLANES, ch + 2 * CONV_HALO - SUBLANES, cc), F32)],
        compiler_params=_cparams(("arbitrary",)),
        name="conv_module",
    )(glu, glu, glu, w, b, lnw, lnb)


def _mix_out_body(d, cc, n_s_tiles, nx, *refs):
    x_refs = refs[:nx]
    mod_ref, ca_ref, as_ref, ap_ref, wo_ref, o_ref = refs[nx:]
    i = pl.program_id(0)
    attn = jnp.where(i < n_s_tiles, as_ref[...], ap_ref[...])
    y = jnp.dot(ca_ref[...], wo_ref[0:cc, :], preferred_element_type=F32)
    y = y + jnp.dot(attn, wo_ref[cc:, :], preferred_element_type=F32)
    g1 = mod_ref[0][:, 2 * d:3 * d]
    o_ref[...] = _x_load(x_refs, i < n_s_tiles) + g1 * y


def _mix_out(g, layer, x, mod, cact, attn_s, attn_p, wo, tm):
    d = g.d
    cc = cact.shape[1]
    na = attn_s.shape[1]
    n_s_tiles = g.ns // tm
    mrow = g.mod_row(layer, tm)
    x_specs, xs = _x_in(g, tm, x)
    return pl.pallas_call(
        functools.partial(_mix_out_body, d, cc, n_s_tiles, len(xs)),
        grid=(g.t // tm,),
        in_specs=x_specs + [
            pl.BlockSpec((1, 1, N_MOD * d), lambda i: (mrow(i), 0, 0)),
            pl.BlockSpec((tm, cc), lambda i: (i, 0)),
            pl.BlockSpec((tm, na), lambda i: (jnp.minimum(i, n_s_tiles - 1), 0)),
            pl.BlockSpec((tm, na), lambda i: (jnp.maximum(i - n_s_tiles, 0), 0)),
            _const_spec((cc + na, d)),
        ],
        out_specs=pl.BlockSpec((tm, d), lambda i: (i, 0)),
        out_shape=jax.ShapeDtypeStruct((g.t, d), F32),
        compiler_params=_cparams(("arbitrary",)),
        name="even_out",
    )(*xs, mod, cact, attn_s, attn_p, wo)


POOL_ROWS = 32


def _pool_body(g, ch, x_ref, xl_ref, xr_ref, mod_ref, n1w_ref, pw_ref, ps_ref, o_ref, hb_ref, mx_ref):
    d = g.d
    pg = d // len(POOL_WINDOWS)
    j, cps = _chunk_pos(g, ch)
    m = mod_ref[0]
    n1w = n1w_ref[...]
    scale = 1.0 + m[:, d:2 * d]
    shift = m[:, 0:d]

    def norm(x):
        return _rms(x, n1w) * scale + shift

    hb_ref[0:POOL_HALO] = jnp.where(j > 0, norm(xl_ref[...]), 0.0)
    hb_ref[POOL_HALO:POOL_HALO + ch] = norm(x_ref[...])
    hb_ref[POOL_HALO + ch:2 * POOL_HALO + ch] = jnp.where(j < cps - 1, norm(xr_ref[...]), 0.0)
    seq_len = cps * ch

    for r0 in range(0, ch, POOL_ROWS):
        pos = j * ch + r0 + lax.broadcasted_iota(jnp.int32, (POOL_ROWS, 1), 0)
        for gi, w in enumerate(POOL_WINDOWS):
            cols = slice(gi * pg, (gi + 1) * pg)
            lo = r0 + POOL_HALO - w // 2
            acc = hb_ref[lo:lo + POOL_ROWS, cols]
            for dd in range(1, w):
                acc = acc + hb_ref[lo + dd:lo + dd + POOL_ROWS, cols]
            cnt = jnp.minimum(pos + w // 2, seq_len) - jnp.maximum(pos - w // 2, 0)
            ctr = hb_ref[r0 + POOL_HALO:r0 + POOL_HALO + POOL_ROWS, cols]
            mx_ref[r0:r0 + POOL_ROWS, cols] = (acc / cnt.astype(F32) - ctr).astype(BF16)

    g1 = m[:, 2 * d:3 * d]
    ps = ps_ref[...]
    for gi in range(len(POOL_WINDOWS)):
        cols = slice(gi * pg, (gi + 1) * pg)
        y = jnp.dot(mx_ref[:, cols], pw_ref[gi], preferred_element_type=F32)
        o_ref[:, cols] = x_ref[:, cols] + g1[:, cols] * (y * ps[:, cols])


def _pool(g, layer, x, mod, n1w, pw, ps, ch):
    d = g.d
    pg = d // len(POOL_WINDOWS)
    main, left, right = _halo_specs(g, ch, POOL_HALO, d)
    mrow = g.mod_row(layer, ch)
    return pl.pallas_call(
        functools.partial(_pool_body, g, ch),
        grid=(g.t // ch,),
        in_specs=[main, left, right,
                  pl.BlockSpec((1, 1, N_MOD * d), lambda n: (mrow(n), 0, 0)),
                  _const_spec((1, d)),
                  _const_spec((len(POOL_WINDOWS), pg, pg)),
                  _const_spec((1, d))],
        out_specs=pl.BlockSpec((ch, d), lambda n: (n, 0)),
        out_shape=jax.ShapeDtypeStruct((g.t, d), F32),
        scratch_shapes=[pltpu.VMEM((ch + 2 * POOL_HALO, d), F32), pltpu.VMEM((ch, d), BF16)],
        compiler_params=_cparams(("arbitrary",)),
        name="pool_mixer",
    )(x, x, x, mod, n1w, pw, ps)


NORM_ROWS = 16


def _ffn_body(d, nj, x_ref, mod_ref, n2w_ref, wg_ref, wu_ref, wd_ref, o_ref, h_ref):
    j = pl.program_id(1)

    @pl.when(j == 0)
    def _():
        m = mod_ref[0]
        w = n2w_ref[...] * (1.0 + m[:, 4 * d:5 * d])
        shift = m[:, 3 * d:4 * d]

        def chunk(r, carry):
            rows = pl.ds(pl.multiple_of(r * NORM_ROWS, NORM_ROWS), NORM_ROWS)
            x = x_ref[rows, :]
            ms = jnp.mean(x * x, axis=-1, keepdims=True)
            h_ref[rows, :] = (x * lax.rsqrt(ms + EPS) * w + shift).astype(BF16)
            o_ref[rows, :] = jnp.zeros((NORM_ROWS, d), F32)
            return carry

        lax.fori_loop(0, x_ref.shape[0] // NORM_ROWS, chunk, 0, unroll=8)

    h = h_ref[...]
    gate = jnp.dot(h, wg_ref[...], preferred_element_type=F32)
    up = jnp.dot(h, wu_ref[...], preferred_element_type=F32)
    a = (_silu(gate) * up).astype(BF16)
    o_ref[...] += jnp.dot(a, wd_ref[...], preferred_element_type=F32)

    @pl.when(j == nj - 1)
    def _():
        o_ref[...] = x_ref[...] + mod_ref[0][:, 5 * d:6 * d] * o_ref[...]


def _ffn(g, layer, x, mod, n2w, wg, wu, wd, tm, tf, tile0=0, ntiles=None):
    d = g.d
    dff = wg.shape[2]
    nj = dff // tf
    ntiles = g.t // tm if ntiles is None else ntiles
    mrow = g.mod_row(layer, tm)
    return pl.pallas_call(
        functools.partial(_ffn_body, d, nj),
        grid=(ntiles, nj),
        in_specs=[
            pl.BlockSpec((tm, d), lambda i, j: (tile0 + i, 0)),
            pl.BlockSpec((1, 1, N_MOD * d), lambda i, j: (mrow(tile0 + i), 0, 0)),
            pl.BlockSpec((1, d), lambda i, j: (0, 0)),
            pl.BlockSpec((None, d, tf), lambda i, j: (layer, 0, j)),
            pl.BlockSpec((None, d, tf), lambda i, j: (layer, 0, j)),
            pl.BlockSpec((None, tf, d), lambda i, j: (layer, j, 0)),
        ],
        out_specs=pl.BlockSpec((tm, d), lambda i, j: (i, 0)),
        out_shape=jax.ShapeDtypeStruct((ntiles * tm, d), F32),
        scratch_shapes=[pltpu.VMEM((tm, d), BF16)],
        compiler_params=_cparams(("arbitrary", "arbitrary")),
        name="ffn",
    )(x, mod, n2w, wg, wu, wd)


def _rope_tables(g, tm):
    rows = g.dec_s // GRID_W
    row = jnp.broadcast_to(jnp.arange(rows, dtype=F32)[:, None], (rows, GRID_W)).reshape(-1)
    col = jnp.broadcast_to(jnp.arange(GRID_W, dtype=F32)[None, :], (rows, GRID_W)).reshape(-1)
    n_freq = QK_ROPE // 4
    inv_freq = 1.0 / (ROPE_THETA ** (jnp.arange(n_freq, dtype=F32) / n_freq))
    ang = jnp.concatenate([row[:, None] * inv_freq, col[:, None] * inv_freq], axis=-1)
    cos, sin = jnp.cos(ang), jnp.sin(ang)
    z = jnp.zeros_like(cos)
    c = jnp.concatenate([cos, z, cos, z], axis=-1)
    s = jnp.concatenate([-sin, z, sin, z], axis=-1)
    c = jnp.concatenate([c, jnp.ones((tm, LANES), F32)], axis=0)
    s = jnp.concatenate([s, jnp.zeros((tm, LANES), F32)], axis=0)
    return c, s


def kernel(x_prompt, x_sample, cache_ckv, cache_kpe, c, c_ctx, norm1_w, norm2_w, w_mod, b_mod, w_in, conv_dw_w, conv_dw_b, conv_ln_w, conv_ln_b, q_a_norm_w, w_q_b, kv_a_norm_w, w_kv_b, q_norm_w, k_norm_w, w_out, pool_w, pool_scale, ffn_w_gate, ffn_w_up, ffn_w_down):
    b, s, d = x_prompt.shape
    dec_b, dec_s, _ = x_sample.shape
    past = cache_ckv.shape[2]
    depth = norm1_w.shape[0]
    g = _Geom(dec_b, dec_s, b, s, d)
    cc = d // 2
    assert dec_b + 1 <= MOD_ROWS and g.ns % s == 0 and dec_s % GRID_W == 0

    tm_proj = min(256, s)
    tm_out = min(512, g.np_, dec_s)
    tm_ffn = min(1024, g.np_, dec_s)
    tf = 512
    ch = s

    cond = jnp.concatenate([c, c_ctx[None, :], jnp.zeros((MOD_ROWS - dec_b - 1, d), F32)], axis=0)
    mod = _modulation(cond, w_mod, b_mod).reshape(depth * MOD_ROWS, 1, N_MOD * d)

    rope_c, rope_s = _rope_tables(g, tm_proj)
    x = (x_sample.reshape(g.ns, d), x_prompt.reshape(g.np_, d))
    wg_all = ffn_w_gate.astype(BF16)
    wu_all = ffn_w_up.astype(BF16)
    wd_all = ffn_w_down.astype(BF16)

    new_ckv, new_kpe = [], []
    for layer in range(depth):
        n1w = norm1_w[layer][None, :]
        if layer % 2 == 0:
            e = layer // 2
            win = jnp.concatenate([w_in[e][:, :2 * cc + Q_LORA + KV_LORA],
                                   _pad_rope(w_in[e][:, 2 * cc + Q_LORA + KV_LORA:])], axis=-1).astype(BF16)
            wq = w_q_b[e].reshape(Q_LORA, MLA_HEADS, QK_HEAD)
            wqb = jnp.concatenate([wq[..., :QK_NOPE], _pad_rope(wq[..., QK_NOPE:])], axis=-1)
            wqb = wqb.reshape(Q_LORA, MLA_HEADS * HEAD_PAD).astype(BF16)
            wkv = w_kv_b[e].reshape(KV_LORA, MLA_HEADS, QK_NOPE + V_HEAD)
            wkvb = jnp.concatenate([wkv[..., :QK_NOPE].reshape(KV_LORA, -1),
                                    wkv[..., QK_NOPE:].reshape(KV_LORA, -1)], axis=-1).astype(BF16)
            qnw = jnp.concatenate([q_norm_w[e][:QK_NOPE], _pad_rope(q_norm_w[e][QK_NOPE:])])[None, :]
            qnw = qnw * _EXP2_SCALE
            knw = jnp.concatenate([k_norm_w[e][:QK_NOPE], _pad_rope(k_norm_w[e][QK_NOPE:])])[None, :]

            glu, q, k, v, ckv, kpe = _proj(
                g, layer, x, mod, n1w, win, q_a_norm_w[e][None, :], wqb, kv_a_norm_w[e][None, :], wkvb,
                qnw, knw, rope_c, rope_s, tm_proj)
            new_ckv.append(ckv[g.ns:].reshape(b, s, KV_LORA))
            new_kpe.append(_unpad_rope(kpe[g.ns:]).reshape(b, s, QK_ROPE))

            kc, vc = _ctx_kv(cache_ckv[:, e].reshape(dec_b * past, KV_LORA).astype(BF16),
                             _pad_rope(cache_kpe[:, e]).reshape(dec_b * past, LANES), wkvb, knw, past)
            attn_s = _attn_sample(g, q, k, v, kc, vc, past)
            attn_p = _attn_prompt(g, q, k, v)
            w_taps = jnp.broadcast_to(conv_dw_w[e][:, None, :], (CONV_K, SUBLANES, cc))
            cact = _conv(g, glu, w_taps, conv_dw_b[e][None, :], conv_ln_w[e][None, :],
                         conv_ln_b[e][None, :], ch)
            x = _mix_out(g, layer, x, mod, cact, attn_s, attn_p, w_out[e].astype(BF16), tm_out)
        else:
            o = layer // 2
            x = _pool(g, layer, x, mod, n1w, pool_w[o].astype(BF16), pool_scale[o][None, :], ch)
        ffn = functools.partial(_ffn, g, layer, x, mod, norm2_w[layer][None, :], wg_all, wu_all, wd_all,
                                tm_ffn, tf)
        if layer < depth - 1:
            x = ffn()
        else:
            n_s_tiles = g.ns // tm_ffn
            x = (ffn(tile0=0, ntiles=n_s_tiles), ffn(tile0=n_s_tiles, ntiles=g.np_ // tm_ffn))

    y_sample = x[0].reshape(dec_b, dec_s, d)
    y_prompt = x[1].reshape(b, s, d)
    return (y_prompt, y_sample, jnp.stack(new_ckv, axis=1), jnp.stack(new_kpe, axis=1))
```

```python
import functools
import math

import jax
import jax.numpy as jnp
from jax import lax
from jax.experimental import pallas as pl
from jax.experimental.pallas import tpu as pltpu

F32 = jnp.float32
BF16 = jnp.bfloat16

EPS = 1e-6
GRID_W = 64
CONV_K = 31
MLA_HEADS = 8
QK_NOPE = 128
QK_ROPE = 64
V_HEAD = 128
QK_HEAD = QK_NOPE + QK_ROPE
Q_LORA = 768
KV_LORA = 512
ROPE_THETA = 10000.0
POOL_WINDOWS = (2, 4, 8, 16)
N_MOD = 6

LANES = 128
SUBLANES = 8
HEAD_PAD = 2 * LANES
V_PAD = 2 * LANES
ROPE_HALF = QK_ROPE // 2
MOD_ROWS = 16
CONV_HALO = 16
POOL_HALO = 8
VMEM_LIMIT = 62 * 1024 * 1024


def _cparams(sem, vmem=VMEM_LIMIT):
    return pltpu.CompilerParams(dimension_semantics=sem, vmem_limit_bytes=vmem)


def _const_spec(shape):
    nd = len(shape)
    return pl.BlockSpec(shape, lambda *_: (0,) * nd, pipeline_mode=pl.Buffered(1))


def _rms(x, w):
    ms = jnp.mean(x * x, axis=-1, keepdims=True)
    return x * lax.rsqrt(ms + EPS) * w


def _silu(x):
    return x * jax.nn.sigmoid(x)


def _pad_rope(w):
    z = jnp.zeros(w.shape[:-1] + (ROPE_HALF,), w.dtype)
    return jnp.concatenate([w[..., 0::2], z, w[..., 1::2], z], axis=-1)


def _unpad_rope(w):
    return jnp.stack([w[..., 0:ROPE_HALF], w[..., 2 * ROPE_HALF:3 * ROPE_HALF]], axis=-1).reshape(
        w.shape[:-1] + (QK_ROPE,))


class _Geom:
    def __init__(self, dec_b, dec_s, b, s, d):
        self.dec_b, self.dec_s, self.b, self.s, self.d = dec_b, dec_s, b, s, d
        self.ns = dec_b * dec_s
        self.np_ = b * s
        self.t = self.ns + self.np_

    def mod_row(self, layer, tile):
        def f(i):
            tok = i * tile
            return layer * MOD_ROWS + jnp.where(tok < self.ns, tok // self.dec_s, self.dec_b)
        return f


def _x_in(g, tm, x, idx=lambda i: i):
    d = g.d
    if not isinstance(x, tuple):
        return [pl.BlockSpec((tm, d), lambda *a: (idx(*a), 0))], (x,)
    n_s = g.ns // tm
    return [pl.BlockSpec((tm, d), lambda *a: (jnp.minimum(idx(*a), n_s - 1), 0)),
            pl.BlockSpec((tm, d), lambda *a: (jnp.maximum(idx(*a) - n_s, 0), 0))], x


def _x_load(x_refs, is_sample):
    if len(x_refs) == 1:
        return x_refs[0][...]
    return jnp.where(is_sample, x_refs[0][...], x_refs[1][...])


def _mod_body(c_ref, w_ref, b_ref, o_ref):
    s = _silu(c_ref[...]).astype(BF16)
    w = w_ref[0].astype(BF16)
    o_ref[0] = jnp.dot(s, w, preferred_element_type=F32) + b_ref[0]


def _modulation(cond, w_mod, b_mod):
    depth, d, n = w_mod.shape
    tn = 1024
    return pl.pallas_call(
        _mod_body,
        grid=(depth, n // tn),
        in_specs=[
            pl.BlockSpec((MOD_ROWS, d), lambda l, j: (0, 0)),
            pl.BlockSpec((1, d, tn), lambda l, j: (l, 0, j)),
            pl.BlockSpec((1, 1, tn), lambda l, j: (l, 0, j)),
        ],
        out_specs=pl.BlockSpec((1, MOD_ROWS, tn), lambda l, j: (l, 0, j)),
        out_shape=jax.ShapeDtypeStruct((depth, MOD_ROWS, n), F32),
        compiler_params=_cparams(("arbitrary", "arbitrary")),
        name="modulation",
    )(cond, w_mod, b_mod.reshape(depth, 1, n))


def _rope128(x, c, s):
    return x * c + pltpu.roll(x, 2 * ROPE_HALF, 1) * s


def _assemble_kv(kv, kpe, knw, rope, k_ref, v_ref):
    hv = MLA_HEADS * QK_NOPE
    kpe_ss = jnp.sum(kpe * kpe, axis=-1, keepdims=True)
    kpe_w = kpe * knw[:, QK_NOPE:]
    if rope is not None:
        kpe_w = _rope128(kpe_w, rope[0], rope[1])
    for h in range(MLA_HEADS):
        kn = kv[:, h * QK_NOPE:(h + 1) * QK_NOPE]
        ss = jnp.sum(kn * kn, axis=-1, keepdims=True) + kpe_ss
        r = lax.rsqrt(ss * (1.0 / QK_HEAD) + EPS)
        k_ref[:, h * HEAD_PAD:h * HEAD_PAD + QK_NOPE] = (kn * r * knw[:, :QK_NOPE]).astype(BF16)
        k_ref[:, h * HEAD_PAD + QK_NOPE:(h + 1) * HEAD_PAD] = (kpe_w * r).astype(BF16)
        v_ref[:, h * V_PAD:h * V_PAD + V_HEAD] = kv[:, hv + h * V_HEAD:hv + (h + 1) * V_HEAD].astype(BF16)
        v_ref[:, h * V_PAD + V_HEAD:(h + 1) * V_PAD] = jnp.ones((kv.shape[0], V_PAD - V_HEAD), BF16)


def _proj_body(d, cc, n_s_tiles, nx, *refs):
    x_refs = refs[:nx]
    (mod_ref, n1w_ref, win_ref, qanw_ref, wqb_ref, kvanw_ref, wkvb_ref, qnw_ref, knw_ref, rc_ref, rs_ref,
     glu_ref, q_ref, k_ref, v_ref, ckv_ref, kpe_ref) = refs[nx:]
    m = mod_ref[0]
    x = _x_load(x_refs, pl.program_id(0) < n_s_tiles)
    h = _rms(x, n1w_ref[...]) * (1.0 + m[:, d:2 * d]) + m[:, 0:d]
    proj = jnp.dot(h.astype(BF16), win_ref[...], preferred_element_type=F32)
    o1 = 2 * cc
    o2 = o1 + Q_LORA
    o3 = o2 + KV_LORA
    glu_ref[...] = proj[:, :cc] * jax.nn.sigmoid(proj[:, cc:o1])

    rc = rc_ref[...]
    rs = rs_ref[...]
    qa = _rms(proj[:, o1:o2], qanw_ref[...]).astype(BF16)
    qb = jnp.dot(qa, wqb_ref[...], preferred_element_type=F32)
    qnw = qnw_ref[...]
    for hd in range(MLA_HEADS):
        qh = qb[:, hd * HEAD_PAD:(hd + 1) * HEAD_PAD]
        ss = jnp.sum(qh * qh, axis=-1, keepdims=True)
        r = lax.rsqrt(ss * (1.0 / QK_HEAD) + EPS)
        qn = qh * r * qnw
        q_ref[:, hd * HEAD_PAD:hd * HEAD_PAD + QK_NOPE] = qn[:, :QK_NOPE].astype(BF16)
        q_ref[:, hd * HEAD_PAD + QK_NOPE:(hd + 1) * HEAD_PAD] = _rope128(qn[:, QK_NOPE:], rc, rs).astype(BF16)

    ckv = _rms(proj[:, o2:o3], kvanw_ref[...])
    ckv_ref[...] = ckv
    kpe = proj[:, o3:]
    kpe_ref[...] = kpe
    kv = jnp.dot(ckv.astype(BF16), wkvb_ref[...], preferred_element_type=F32)
    _assemble_kv(kv, kpe, knw_ref[...], (rc, rs), k_ref, v_ref)


def _proj(g, layer, x, mod, n1w, win, qanw, wqb, kvanw, wkvb, qnw, knw, rope_c, rope_s, tm):
    d = g.d
    cc = d // 2
    nin = win.shape[1]
    nq = MLA_HEADS * HEAD_PAD
    nv = MLA_HEADS * V_PAD
    s_tiles = g.dec_s // tm

    def rope_idx(i):
        return (jnp.where(i * tm < g.ns, i % s_tiles, s_tiles), 0)

    row = lambda n: pl.BlockSpec((tm, n), lambda i: (i, 0))
    mrow = g.mod_row(layer, tm)
    x_specs, xs = _x_in(g, tm, x)
    return pl.pallas_call(
        functools.partial(_proj_body, d, cc, g.ns // tm, len(xs)),
        grid=(g.t // tm,),
        in_specs=x_specs + [
            pl.BlockSpec((1, 1, N_MOD * d), lambda i: (mrow(i), 0, 0)),
            _const_spec((1, d)),
            _const_spec((d, nin)),
            _const_spec((1, Q_LORA)),
            _const_spec((Q_LORA, nq)),
            _const_spec((1, KV_LORA)),
            _const_spec((KV_LORA, nq)),
            _const_spec((1, HEAD_PAD)),
            _const_spec((1, HEAD_PAD)),
            pl.BlockSpec((tm, LANES), rope_idx),
            pl.BlockSpec((tm, LANES), rope_idx),
        ],
        out_specs=[row(cc), row(nq), row(nq), row(nv), row(KV_LORA), row(LANES)],
        out_shape=[
            jax.ShapeDtypeStruct((g.t, cc), F32),
            jax.ShapeDtypeStruct((g.t, nq), BF16),
            jax.ShapeDtypeStruct((g.t, nq), BF16),
            jax.ShapeDtypeStruct((g.t, nv), BF16),
            jax.ShapeDtypeStruct((g.t, KV_LORA), F32),
            jax.ShapeDtypeStruct((g.t, LANES), F32),
        ],
        compiler_params=_cparams(("arbitrary",)),
        name="even_proj",
    )(*xs, mod, n1w, win, qanw, wqb, kvanw, wkvb, qnw, knw, rope_c, rope_s)


def _ctx_kv_body(ckv_ref, kpe_ref, wkvb_ref, knw_ref, k_ref, v_ref):
    kv = jnp.dot(ckv_ref[...], wkvb_ref[...], preferred_element_type=F32)
    _assemble_kv(kv, kpe_ref[...], knw_ref[...], None, k_ref, v_ref)


def _ctx_kv(ckv, kpe, wkvb, knw, tm):
    n = ckv.shape[0]
    nq = MLA_HEADS * HEAD_PAD
    nv = MLA_HEADS * V_PAD
    row = lambda w: pl.BlockSpec((tm, w), lambda i: (i, 0))
    return pl.pallas_call(
        _ctx_kv_body,
        grid=(n // tm,),
        in_specs=[row(KV_LORA), row(LANES), _const_spec((KV_LORA, nq)), _const_spec((1, HEAD_PAD))],
        out_specs=[row(nq), row(nv)],
        out_shape=[jax.ShapeDtypeStruct((n, nq), BF16), jax.ShapeDtypeStruct((n, nv), BF16)],
        compiler_params=_cparams(("arbitrary",)),
        name="ctx_kv",
    )(ckv, kpe, wkvb, knw)


_NT = (((1,), (1,)), ((), ()))
_EXP2_SCALE = math.log2(math.e) * QK_HEAD ** -0.5

ATTN_SUB = 512
ATTN_KCHUNK = 1024


def _normalise(o):
    return (o[:, :V_HEAD] / o[:, V_HEAD:]).astype(BF16)


def _attn_sample_body(q_ref, kc_ref, kl_ref, vc_ref, vl_ref, o_ref, s0_ref, s1_ref):
    past = kc_ref.shape[0]
    nk = kl_ref.shape[0]
    sub = s0_ref.shape[0]
    nsub = q_ref.shape[0] // sub

    def rows(i):
        return pl.ds(pl.multiple_of(i * sub, sub), sub)

    def scores(i, s_ref):
        q = q_ref[rows(i), :]
        s_ref[:, :past] = lax.dot_general(q, kc_ref[...], _NT, preferred_element_type=F32)
        s_ref[:, past:] = lax.dot_general(q, kl_ref[...], _NT, preferred_element_type=F32)

    def finish(i, s_ref):
        m = jnp.max(s_ref[...], axis=-1, keepdims=True)
        p = jnp.exp2(s_ref[:, :past] - m)
        o = jnp.dot(p.astype(BF16), vc_ref[...], preferred_element_type=F32)
        kchunk = min(ATTN_KCHUNK, nk)
        for c0 in range(0, nk, kchunk):
            p = jnp.exp2(s_ref[:, past + c0:past + c0 + kchunk] - m)
            o = o + jnp.dot(p.astype(BF16), vl_ref[c0:c0 + kchunk, :], preferred_element_type=F32)
        o_ref[rows(i), :] = _normalise(o)

    scores(0, s0_ref)

    def pair(t, carry):
        i = 2 * t
        scores(i + 1, s1_ref)
        finish(i, s0_ref)
        scores(i + 2, s0_ref)
        finish(i + 1, s1_ref)
        return carry

    lax.fori_loop(0, nsub // 2 - 1, pair, 0)
    scores(nsub - 1, s1_ref)
    finish(nsub - 2, s0_ref)
    finish(nsub - 1, s1_ref)


def _attn_sample(g, q, k, v, kc, vc, past):
    sub = min(ATTN_SUB, g.dec_s // 2)
    assert g.dec_s % (2 * sub) == 0 and g.dec_s % min(ATTN_KCHUNK, g.dec_s) == 0
    score_buf = pltpu.VMEM((sub, past + g.dec_s), F32)
    return pl.pallas_call(
        _attn_sample_body,
        grid=(g.dec_b, MLA_HEADS),
        in_specs=[
            pl.BlockSpec((g.dec_s, HEAD_PAD), lambda b, h: (b, h)),
            pl.BlockSpec((past, HEAD_PAD), lambda b, h: (b, h)),
            pl.BlockSpec((g.dec_s, HEAD_PAD), lambda b, h: (b, h)),
            pl.BlockSpec((past, V_PAD), lambda b, h: (b, h)),
            pl.BlockSpec((g.dec_s, V_PAD), lambda b, h: (b, h)),
        ],
        out_specs=pl.BlockSpec((g.dec_s, V_HEAD), lambda b, h: (b, h)),
        out_shape=jax.ShapeDtypeStruct((g.ns, MLA_HEADS * V_HEAD), BF16),
        scratch_shapes=[score_buf, score_buf],
        compiler_params=_cparams(("arbitrary", "arbitrary")),
        name="attn_sample",
    )(q, kc, k, vc, v)


def _attn_prompt_body(q_ref, k_ref, v_ref, o_ref):
    for h in range(MLA_HEADS):
        qk = slice(h * HEAD_PAD, (h + 1) * HEAD_PAD)
        s = lax.dot_general(q_ref[:, qk], k_ref[:, qk], _NT, preferred_element_type=F32)
        m = jnp.max(s, axis=-1, keepdims=True)
        p = jnp.exp2(s - m)
        o = jnp.dot(p.astype(BF16), v_ref[:, h * V_PAD:(h + 1) * V_PAD], preferred_element_type=F32)
        o_ref[:, h * V_HEAD:(h + 1) * V_HEAD] = _normalise(o)


def _attn_prompt(g, q, k, v):
    off = g.ns // g.s
    return pl.pallas_call(
        _attn_prompt_body,
        grid=(g.b,),
        in_specs=[
            pl.BlockSpec((g.s, MLA_HEADS * HEAD_PAD), lambda b: (off + b, 0)),
            pl.BlockSpec((g.s, MLA_HEADS * HEAD_PAD), lambda b: (off + b, 0)),
            pl.BlockSpec((g.s, MLA_HEADS * V_PAD), lambda b: (off + b, 0)),
        ],
        out_specs=pl.BlockSpec((g.s, MLA_HEADS * V_HEAD), lambda b: (b, 0)),
        out_shape=jax.ShapeDtypeStruct((g.np_, MLA_HEADS * V_HEAD), BF16),
        compiler_params=_cparams(("arbitrary",)),
        name="attn_prompt",
    )(q, k, v)


def _chunk_pos(g, ch):
    n = pl.program_id(0)
    n_s = g.ns // ch
    cps_s = g.dec_s // ch
    cps_p = g.s // ch
    is_s = n < n_s
    j = jnp.where(is_s, n % cps_s, (n - n_s) % cps_p)
    cps = jnp.where(is_s, cps_s, cps_p)
    return j, cps


def _halo_specs(g, ch, halo, width):
    per = ch // halo
    last = g.t // halo - 1
    main = pl.BlockSpec((ch, width), lambda n: (n, 0))
    left = pl.BlockSpec((halo, width), lambda n: (jnp.maximum(n * per - 1, 0), 0))
    right = pl.BlockSpec((halo, width), lambda n: (jnp.minimum((n + 1) * per, last), 0))
    return main, left, right


CONV_ROWS = 32


def _conv_body(g, ch, x_ref, xl_ref, xr_ref, w_ref, b_ref, lnw_ref, lnb_ref, o_ref, xp_ref, xs_ref):
    j, cps = _chunk_pos(g, ch)
    xp_ref[0:CONV_HALO] = jnp.where(j > 0, xl_ref[...], 0.0)
    xp_ref[CONV_HALO:CONV_HALO + ch] = x_ref[...]
    xp_ref[CONV_HALO + ch:2 * CONV_HALO + ch] = jnp.where(j < cps - 1, xr_ref[...], 0.0)
    bias = b_ref[...]
    lnw = lnw_ref[...]
    lnb = lnb_ref[...]
    base = CONV_HALO - CONV_K // 2

    span = xs_ref.shape[1]
    for ph in range(1, SUBLANES):
        xs_ref[ph] = xp_ref[ph:ph + span, :]

    def src(lo, ph):
        return xp_ref[lo:lo + SUBLANES, :] if ph == 0 else xs_ref[ph, lo:lo + SUBLANES, :]

    for r0 in range(0, ch, CONV_ROWS):
        accs = [None] * (CONV_ROWS // SUBLANES)
        for k in range(CONV_K):
            wk = w_ref[k]
            ph = (base + k) % SUBLANES
            for jj in range(len(accs)):
                term = src(r0 + jj * SUBLANES + base + k - ph, ph) * wk
                accs[jj] = term if accs[jj] is None else accs[jj] + term
        acc = jnp.concatenate(accs, axis=0) + bias
        mu = jnp.mean(acc, axis=-1, keepdims=True)
        xc = acc - mu
        var = jnp.mean(xc * xc, axis=-1, keepdims=True)
        y = xc * lax.rsqrt(var + EPS) * lnw + lnb
        o_ref[r0:r0 + CONV_ROWS, :] = _silu(y).astype(BF16)


def _conv(g, glu, w, b, lnw, lnb, ch):
    cc = glu.shape[1]
    main, left, right = _halo_specs(g, ch, CONV_HALO, cc)
    return pl.pallas_call(
        functools.partial(_conv_body, g, ch),
        grid=(g.t // ch,),
        in_specs=[main, left, right, _const_spec((CONV_K, SUBLANES, cc)), _const_spec((1, cc)),
                  _const_spec((1, cc)), _const_spec((1, cc))],
        out_specs=pl.BlockSpec((ch, cc), lambda n: (n, 0)),
        out_shape=jax.ShapeDtypeStruct((g.t, cc), BF16),
        scratch_shapes=[pltpu.VMEM((ch + 2 * CONV_HALO, cc), F32),
                        pltpu.VMEM((SUBLANES, ch + 2 * CONV_HALO - SUBLANES, cc), F32)],
        compiler_params=_cparams(("arbitrary",)),
        name="conv_module",
    )(glu, glu, glu, w, b, lnw, lnb)


def _mix_out_body(d, cc, n_s_tiles, nx, *refs):
    x_refs = refs[:nx]
    mod_ref, ca_ref, as_ref, ap_ref, wo_ref, o_ref = refs[nx:]
    i = pl.program_id(0)
    attn = jnp.where(i < n_s_tiles, as_ref[...], ap_ref[...])
    y = jnp.dot(ca_ref[...], wo_ref[0:cc, :], preferred_element_type=F32)
    y = y + jnp.dot(attn, wo_ref[cc:, :], preferred_element_type=F32)
    g1 = mod_ref[0][:, 2 * d:3 * d]
    o_ref[...] = _x_load(x_refs, i < n_s_tiles) + g1 * y


def _mix_out(g, layer, x, mod, cact, attn_s, attn_p, wo, tm):
    d = g.d
    cc = cact.shape[1]
    na = attn_s.shape[1]
    n_s_tiles = g.ns // tm
    mrow = g.mod_row(layer, tm)
    x_specs, xs = _x_in(g, tm, x)
    return pl.pallas_call(
        functools.partial(_mix_out_body, d, cc, n_s_tiles, len(xs)),
        grid=(g.t // tm,),
        in_specs=x_specs + [
            pl.BlockSpec((1, 1, N_MOD * d), lambda i: (mrow(i), 0, 0)),
            pl.BlockSpec((tm, cc), lambda i: (i, 0)),
            pl.BlockSpec((tm, na), lambda i: (jnp.minimum(i, n_s_tiles - 1), 0)),
            pl.BlockSpec((tm, na), lambda i: (jnp.maximum(i - n_s_tiles, 0), 0)),
            _const_spec((cc + na, d)),
        ],
        out_specs=pl.BlockSpec((tm, d), lambda i: (i, 0)),
        out_shape=jax.ShapeDtypeStruct((g.t, d), F32),
        compiler_params=_cparams(("arbitrary",)),
        name="even_out",
    )(*xs, mod, cact, attn_s, attn_p, wo)


def _pool_body(g, ch, x_ref, xl_ref, xr_ref, mod_ref, n1w_ref, pw_ref, ps_ref, o_ref, hb_ref, mx_ref,
               *lvl_refs):
    d = g.d
    pg = d // len(POOL_WINDOWS)
    j, cps = _chunk_pos(g, ch)
    m = mod_ref[0]
    n1w = n1w_ref[...]
    scale = 1.0 + m[:, d:2 * d]
    shift = m[:, 0:d]

    def norm(x):
        return _rms(x, n1w) * scale + shift

    pad = POOL_HALO
    base = pad + POOL_HALO
    total = base + ch + POOL_HALO
    hb_ref[0:pad] = jnp.zeros((pad, d), F32)
    hb_ref[pad:base] = jnp.where(j > 0, norm(xl_ref[...]), 0.0)
    hb_ref[base:base + ch] = norm(x_ref[...])
    hb_ref[base + ch:total] = jnp.where(j < cps - 1, norm(xr_ref[...]), 0.0)
    seq_len = cps * ch

    levels = (hb_ref,) + tuple(lvl_refs)
    for k in range(1, len(levels)):
        src, step = levels[k - 1], 1 << (k - 1)
        levels[k][0:pad] = jnp.zeros((pad, levels[k].shape[1]), F32)
        levels[k][pad:total] = src[pad:total, pg:] + src[pad - step:total - step, pg:]

    pos = j * ch + lax.broadcasted_iota(jnp.int32, (ch, 1), 0)
    for gi, w in enumerate(POOL_WINDOWS):
        half = w // 2
        assert half == 1 << gi
        cols = slice(gi * pg, (gi + 1) * pg)
        src = levels[gi]
        win = src[base - 1:base - 1 + ch, 0:pg] + src[base + half - 1:base + half - 1 + ch, 0:pg]
        cnt = jnp.minimum(pos + half, seq_len) - jnp.maximum(pos - half, 0)
        mx_ref[:, cols] = (win / cnt.astype(F32) - hb_ref[base:base + ch, cols]).astype(BF16)

    g1 = m[:, 2 * d:3 * d]
    ps = ps_ref[...]
    for gi in range(len(POOL_WINDOWS)):
        cols = slice(gi * pg, (gi + 1) * pg)
        y = jnp.dot(mx_ref[:, cols], pw_ref[gi], preferred_element_type=F32)
        o_ref[:, cols] = x_ref[:, cols] + g1[:, cols] * (y * ps[:, cols])


def _pool(g, layer, x, mod, n1w, pw, ps, ch):
    d = g.d
    pg = d // len(POOL_WINDOWS)
    main, left, right = _halo_specs(g, ch, POOL_HALO, d)
    mrow = g.mod_row(layer, ch)
    return pl.pallas_call(
        functools.partial(_pool_body, g, ch),
        grid=(g.t // ch,),
        in_specs=[main, left, right,
                  pl.BlockSpec((1, 1, N_MOD * d), lambda n: (mrow(n), 0, 0)),
                  _const_spec((1, d)),
                  _const_spec((len(POOL_WINDOWS), pg, pg)),
                  _const_spec((1, d))],
        out_specs=pl.BlockSpec((ch, d), lambda n: (n, 0)),
        out_shape=jax.ShapeDtypeStruct((g.t, d), F32),
        scratch_shapes=[pltpu.VMEM((ch + 3 * POOL_HALO, d), F32), pltpu.VMEM((ch, d), BF16)]
        + [pltpu.VMEM((ch + 3 * POOL_HALO, d - k * pg), F32) for k in range(1, len(POOL_WINDOWS))],
        compiler_params=_cparams(("arbitrary",)),
        name="pool_mixer",
    )(x, x, x, mod, n1w, pw, ps)


def _ffn_body(d, nj, x_ref, mod_ref, n2w_ref, wg_ref, wu_ref, wd_ref, o_ref, h_ref):
    j = pl.program_id(1)
    m = mod_ref[0]

    def gated_chunk(h):
        gate = jnp.dot(h, wg_ref[...], preferred_element_type=F32)
        up = jnp.dot(h, wu_ref[...], preferred_element_type=F32)
        a = (_silu(gate) * up).astype(BF16)
        return m[:, 5 * d:6 * d] * jnp.dot(a, wd_ref[...], preferred_element_type=F32)

    @pl.when(j == 0)
    def _():
        x = x_ref[...]
        h = (_rms(x, n2w_ref[...]) * (1.0 + m[:, 4 * d:5 * d]) + m[:, 3 * d:4 * d]).astype(BF16)
        h_ref[...] = h
        o_ref[...] = x + gated_chunk(h)

    @pl.when(j > 0)
    def _():
        o_ref[...] += gated_chunk(h_ref[...])


def _ffn(g, layer, x, mod, n2w, wg, wu, wd, tm, tf, tile0=0, ntiles=None):
    d = g.d
    dff = wg.shape[2]
    nj = dff // tf
    ntiles = g.t // tm if ntiles is None else ntiles
    mrow = g.mod_row(layer, tm)
    return pl.pallas_call(
        functools.partial(_ffn_body, d, nj),
        grid=(ntiles, nj),
        in_specs=[
            pl.BlockSpec((tm, d), lambda i, j: (tile0 + i, 0)),
            pl.BlockSpec((1, 1, N_MOD * d), lambda i, j: (mrow(tile0 + i), 0, 0)),
            pl.BlockSpec((1, d), lambda i, j: (0, 0)),
            pl.BlockSpec((None, d, tf), lambda i, j: (layer, 0, j)),
            pl.BlockSpec((None, d, tf), lambda i, j: (layer, 0, j)),
            pl.BlockSpec((None, tf, d), lambda i, j: (layer, j, 0)),
        ],
        out_specs=pl.BlockSpec((tm, d), lambda i, j: (i, 0)),
        out_shape=jax.ShapeDtypeStruct((ntiles * tm, d), F32),
        scratch_shapes=[pltpu.VMEM((tm, d), BF16)],
        compiler_params=_cparams(("arbitrary", "arbitrary")),
        name="ffn",
    )(x, mod, n2w, wg, wu, wd)


def _rope_tables(g, tm):
    rows = g.dec_s // GRID_W
    row = jnp.broadcast_to(jnp.arange(rows, dtype=F32)[:, None], (rows, GRID_W)).reshape(-1)
    col = jnp.broadcast_to(jnp.arange(GRID_W, dtype=F32)[None, :], (rows, GRID_W)).reshape(-1)
    n_freq = QK_ROPE // 4
    inv_freq = 1.0 / (ROPE_THETA ** (jnp.arange(n_freq, dtype=F32) / n_freq))
    ang = jnp.concatenate([row[:, None] * inv_freq, col[:, None] * inv_freq], axis=-1)
    cos, sin = jnp.cos(ang), jnp.sin(ang)
    z = jnp.zeros_like(cos)
    c = jnp.concatenate([cos, z, cos, z], axis=-1)
    s = jnp.concatenate([-sin, z, sin, z], axis=-1)
    c = jnp.concatenate([c, jnp.ones((tm, LANES), F32)], axis=0)
    s = jnp.concatenate([s, jnp.zeros((tm, LANES), F32)], axis=0)
    return c, s


def kernel(x_prompt, x_sample, cache_ckv, cache_kpe, c, c_ctx, norm1_w, norm2_w, w_mod, b_mod, w_in, conv_dw_w, conv_dw_b, conv_ln_w, conv_ln_b, q_a_norm_w, w_q_b, kv_a_norm_w, w_kv_b, q_norm_w, k_norm_w, w_out, pool_w, pool_scale, ffn_w_gate, ffn_w_up, ffn_w_down):
    b, s, d = x_prompt.shape
    dec_b, dec_s, _ = x_sample.shape
    past = cache_ckv.shape[2]
    depth = norm1_w.shape[0]
    g = _Geom(dec_b, dec_s, b, s, d)
    cc = d // 2
    assert dec_b + 1 <= MOD_ROWS and g.ns % s == 0 and dec_s % GRID_W == 0

    tm_proj = min(256, s)
    tm_out = min(512, g.np_, dec_s)
    tm_ffn = min(1024, g.np_, dec_s)
    tf = 512
    ch = s

    cond = jnp.concatenate([c, c_ctx[None, :], jnp.zeros((MOD_ROWS - dec_b - 1, d), F32)], axis=0)
    mod = _modulation(cond, w_mod, b_mod).reshape(depth * MOD_ROWS, 1, N_MOD * d)

    rope_c, rope_s = _rope_tables(g, tm_proj)
    x = (x_sample.reshape(g.ns, d), x_prompt.reshape(g.np_, d))
    wg_all = ffn_w_gate.astype(BF16)
    wu_all = ffn_w_up.astype(BF16)
    wd_all = ffn_w_down.astype(BF16)

    new_ckv, new_kpe = [], []
    for layer in range(depth):
        n1w = norm1_w[layer][None, :]
        if layer % 2 == 0:
            e = layer // 2
            win = jnp.concatenate([w_in[e][:, :2 * cc + Q_LORA + KV_LORA],
                                   _pad_rope(w_in[e][:, 2 * cc + Q_LORA + KV_LORA:])], axis=-1).astype(BF16)
            wq = w_q_b[e].reshape(Q_LORA, MLA_HEADS, QK_HEAD)
            wqb = jnp.concatenate([wq[..., :QK_NOPE], _pad_rope(wq[..., QK_NOPE:])], axis=-1)
            wqb = wqb.reshape(Q_LORA, MLA_HEADS * HEAD_PAD).astype(BF16)
            wkv = w_kv_b[e].reshape(KV_LORA, MLA_HEADS, QK_NOPE + V_HEAD)
            wkvb = jnp.concatenate([wkv[..., :QK_NOPE].reshape(KV_LORA, -1),
                                    wkv[..., QK_NOPE:].reshape(KV_LORA, -1)], axis=-1).astype(BF16)
            qnw = jnp.concatenate([q_norm_w[e][:QK_NOPE], _pad_rope(q_norm_w[e][QK_NOPE:])])[None, :]
            qnw = qnw * _EXP2_SCALE
            knw = jnp.concatenate([k_norm_w[e][:QK_NOPE], _pad_rope(k_norm_w[e][QK_NOPE:])])[None, :]

            glu, q, k, v, ckv, kpe = _proj(
                g, layer, x, mod, n1w, win, q_a_norm_w[e][None, :], wqb, kv_a_norm_w[e][None, :], wkvb,
                qnw, knw, rope_c, rope_s, tm_proj)
            new_ckv.append(ckv[g.ns:].reshape(b, s, KV_LORA))
            new_kpe.append(_unpad_rope(kpe[g.ns:]).reshape(b, s, QK_ROPE))

            kc, vc = _ctx_kv(cache_ckv[:, e].reshape(dec_b * past, KV_LORA).astype(BF16),
                             _pad_rope(cache_kpe[:, e]).reshape(dec_b * past, LANES), wkvb, knw, past)
            attn_s = _attn_sample(g, q, k, v, kc, vc, past)
            attn_p = _attn_prompt(g, q, k, v)
            w_taps = jnp.broadcast_to(conv_dw_w[e][:, None, :], (CONV_K, SUBLANES, cc))
            cact = _conv(g, glu, w_taps, conv_dw_b[e][None, :], conv_ln_w[e][None, :],
                         conv_ln_b[e][None, :], ch)
            x = _mix_out(g, layer, x, mod, cact, attn_s, attn_p, w_out[e].astype(BF16), tm_out)
        else:
            o = layer // 2
            x = _pool(g, layer, x, mod, n1w, pool_w[o].astype(BF16), pool_scale[o][None, :], ch)
        ffn = functools.partial(_ffn, g, layer, x, mod, norm2_w[layer][None, :], wg_all, wu_all, wd_all,
                                tm_ffn, tf)
        if layer < depth - 1:
            x = ffn()
        else:
            n_s_tiles = g.ns // tm_ffn
            x = (ffn(tile0=0, ntiles=n_s_tiles), ffn(tile0=n_s_tiles, ntiles=g.np_ // tm_ffn))

    y_sample = x[0].reshape(dec_b, dec_s, d)
    y_prompt = x[1].reshape(b, s, d)
    return (y_prompt, y_sample, jnp.stack(new_ckv, axis=1), jnp.stack(new_kpe, axis=1))
```

```python
import functools
import math

import jax
import jax.numpy as jnp
from jax import lax
from jax.experimental import pallas as pl
from jax.experimental.pallas import tpu as pltpu

F32 = jnp.float32
BF16 = jnp.bfloat16

EPS = 1e-6
GRID_W = 64
CONV_K = 31
MLA_HEADS = 8
QK_NOPE = 128
QK_ROPE = 64
V_HEAD = 128
QK_HEAD = QK_NOPE + QK_ROPE
Q_LORA = 768
KV_LORA = 512
ROPE_THETA = 10000.0
POOL_WINDOWS = (2, 4, 8, 16)
N_MOD = 6

LANES = 128
SUBLANES = 8
HEAD_PAD = 2 * LANES
ONES_ROWS = 16
ROPE_HALF = QK_ROPE // 2
MOD_ROWS = 16
CONV_HALO = 16
POOL_HALO = 8
VMEM_LIMIT = 62 * 1024 * 1024


def _cparams(sem, vmem=VMEM_LIMIT):
    return pltpu.CompilerParams(dimension_semantics=sem, vmem_limit_bytes=vmem)


def _const_spec(shape):
    nd = len(shape)
    return pl.BlockSpec(shape, lambda *_: (0,) * nd, pipeline_mode=pl.Buffered(1))


def _rms(x, w):
    ms = jnp.mean(x * x, axis=-1, keepdims=True)
    return x * lax.rsqrt(ms + EPS) * w


def _silu(x):
    return x * jax.nn.sigmoid(x)


def _pad_rope(w):
    z = jnp.zeros(w.shape[:-1] + (ROPE_HALF,), w.dtype)
    return jnp.concatenate([w[..., 0::2], z, w[..., 1::2], z], axis=-1)


def _unpad_rope(w):
    return jnp.stack([w[..., 0:ROPE_HALF], w[..., 2 * ROPE_HALF:3 * ROPE_HALF]], axis=-1).reshape(
        w.shape[:-1] + (QK_ROPE,))


class _Geom:
    def __init__(self, dec_b, dec_s, b, s, d):
        self.dec_b, self.dec_s, self.b, self.s, self.d = dec_b, dec_s, b, s, d
        self.ns = dec_b * dec_s
        self.np_ = b * s
        self.t = self.ns + self.np_

    def mod_row(self, layer, tile):
        def f(i):
            tok = i * tile
            return layer * MOD_ROWS + jnp.where(tok < self.ns, tok // self.dec_s, self.dec_b)
        return f


def _x_in(g, tm, x, idx=lambda i: i):
    d = g.d
    if not isinstance(x, tuple):
        return [pl.BlockSpec((tm, d), lambda *a: (idx(*a), 0))], (x,)
    n_s = g.ns // tm
    return [pl.BlockSpec((tm, d), lambda *a: (jnp.minimum(idx(*a), n_s - 1), 0)),
            pl.BlockSpec((tm, d), lambda *a: (jnp.maximum(idx(*a) - n_s, 0), 0))], x


def _x_load(x_refs, is_sample):
    if len(x_refs) == 1:
        return x_refs[0][...]
    return jnp.where(is_sample, x_refs[0][...], x_refs[1][...])


def _mod_body(c_ref, w_ref, b_ref, o_ref):
    s = _silu(c_ref[...]).astype(BF16)
    w = w_ref[0].astype(BF16)
    o_ref[0] = jnp.dot(s, w, preferred_element_type=F32) + b_ref[0]


def _modulation(cond, w_mod, b_mod):
    depth, d, n = w_mod.shape
    tn = 1024
    return pl.pallas_call(
        _mod_body,
        grid=(depth, n // tn),
        in_specs=[
            pl.BlockSpec((MOD_ROWS, d), lambda l, j: (0, 0)),
            pl.BlockSpec((1, d, tn), lambda l, j: (l, 0, j)),
            pl.BlockSpec((1, 1, tn), lambda l, j: (l, 0, j)),
        ],
        out_specs=pl.BlockSpec((1, MOD_ROWS, tn), lambda l, j: (l, 0, j)),
        out_shape=jax.ShapeDtypeStruct((depth, MOD_ROWS, n), F32),
        compiler_params=_cparams(("arbitrary", "arbitrary")),
        name="modulation",
    )(cond, w_mod, b_mod.reshape(depth, 1, n))


def _rope128(x, c, s):
    return x * c + pltpu.roll(x, 2 * ROPE_HALF, 1) * s


def _assemble_kv(kv, kpe, knw, rope, k_ref, vt_ref):
    hv = MLA_HEADS * QK_NOPE
    kpe_ss = jnp.sum(kpe * kpe, axis=-1, keepdims=True)
    kpe_w = kpe * knw[:, QK_NOPE:]
    if rope is not None:
        kpe_w = _rope128(kpe_w, rope[0], rope[1])
    for h in range(MLA_HEADS):
        kn = kv[:, h * QK_NOPE:(h + 1) * QK_NOPE]
        ss = jnp.sum(kn * kn, axis=-1, keepdims=True) + kpe_ss
        r = lax.rsqrt(ss * (1.0 / QK_HEAD) + EPS)
        k_ref[:, h * HEAD_PAD:h * HEAD_PAD + QK_NOPE] = (kn * r * knw[:, :QK_NOPE]).astype(BF16)
        k_ref[:, h * HEAD_PAD + QK_NOPE:(h + 1) * HEAD_PAD] = (kpe_w * r).astype(BF16)
    vt_ref[...] = kv[:, hv:].T.astype(BF16)


def _proj_body(d, cc, n_s_tiles, nx, *refs):
    x_refs = refs[:nx]
    (mod_ref, n1w_ref, win_ref, qanw_ref, wqb_ref, kvanw_ref, wkvb_ref, qnw_ref, knw_ref, rc_ref, rs_ref,
     glu_ref, q_ref, k_ref, v_ref, ckv_ref, kpe_ref) = refs[nx:]
    m = mod_ref[0]
    x = _x_load(x_refs, pl.program_id(0) < n_s_tiles)
    h = _rms(x, n1w_ref[...]) * (1.0 + m[:, d:2 * d]) + m[:, 0:d]
    proj = jnp.dot(h.astype(BF16), win_ref[...], preferred_element_type=F32)
    o1 = 2 * cc
    o2 = o1 + Q_LORA
    o3 = o2 + KV_LORA
    glu_ref[...] = proj[:, :cc] * jax.nn.sigmoid(proj[:, cc:o1])

    rc = rc_ref[...]
    rs = rs_ref[...]
    qa = _rms(proj[:, o1:o2], qanw_ref[...]).astype(BF16)
    qb = jnp.dot(qa, wqb_ref[...], preferred_element_type=F32)
    qnw = qnw_ref[...]
    for hd in range(MLA_HEADS):
        qh = qb[:, hd * HEAD_PAD:(hd + 1) * HEAD_PAD]
        ss = jnp.sum(qh * qh, axis=-1, keepdims=True)
        r = lax.rsqrt(ss * (1.0 / QK_HEAD) + EPS)
        qn = qh * r * qnw
        q_ref[:, hd * HEAD_PAD:hd * HEAD_PAD + QK_NOPE] = qn[:, :QK_NOPE].astype(BF16)
        q_ref[:, hd * HEAD_PAD + QK_NOPE:(hd + 1) * HEAD_PAD] = _rope128(qn[:, QK_NOPE:], rc, rs).astype(BF16)

    ckv = _rms(proj[:, o2:o3], kvanw_ref[...])
    ckv_ref[...] = ckv
    kpe = proj[:, o3:]
    kpe_ref[...] = kpe
    kv = jnp.dot(ckv.astype(BF16), wkvb_ref[...], preferred_element_type=F32)
    _assemble_kv(kv, kpe, knw_ref[...], (rc, rs), k_ref, v_ref)


def _proj(g, layer, x, mod, n1w, win, qanw, wqb, kvanw, wkvb, qnw, knw, rope_c, rope_s, tm):
    d = g.d
    cc = d // 2
    nin = win.shape[1]
    nq = MLA_HEADS * HEAD_PAD
    nv = MLA_HEADS * V_HEAD
    s_tiles = g.dec_s // tm

    def rope_idx(i):
        return (jnp.where(i * tm < g.ns, i % s_tiles, s_tiles), 0)

    row = lambda n: pl.BlockSpec((tm, n), lambda i: (i, 0))
    mrow = g.mod_row(layer, tm)
    x_specs, xs = _x_in(g, tm, x)
    return pl.pallas_call(
        functools.partial(_proj_body, d, cc, g.ns // tm, len(xs)),
        grid=(g.t // tm,),
        in_specs=x_specs + [
            pl.BlockSpec((1, 1, N_MOD * d), lambda i: (mrow(i), 0, 0)),
            _const_spec((1, d)),
            _const_spec((d, nin)),
            _const_spec((1, Q_LORA)),
            _const_spec((Q_LORA, nq)),
            _const_spec((1, KV_LORA)),
            _const_spec((KV_LORA, nq)),
            _const_spec((1, HEAD_PAD)),
            _const_spec((1, HEAD_PAD)),
            pl.BlockSpec((tm, LANES), rope_idx),
            pl.BlockSpec((tm, LANES), rope_idx),
        ],
        out_specs=[row(cc), row(nq), row(nq), pl.BlockSpec((nv, tm), lambda i: (0, i)), row(KV_LORA),
                   row(LANES)],
        out_shape=[
            jax.ShapeDtypeStruct((g.t, cc), F32),
            jax.ShapeDtypeStruct((g.t, nq), BF16),
            jax.ShapeDtypeStruct((g.t, nq), BF16),
            jax.ShapeDtypeStruct((nv, g.t), BF16),
            jax.ShapeDtypeStruct((g.t, KV_LORA), F32),
            jax.ShapeDtypeStruct((g.t, LANES), F32),
        ],
        compiler_params=_cparams(("arbitrary",)),
        name="even_proj",
    )(*xs, mod, n1w, win, qanw, wqb, kvanw, wkvb, qnw, knw, rope_c, rope_s)


def _ctx_kv_body(ckv_ref, kpe_ref, wkvb_ref, knw_ref, k_ref, v_ref):
    kv = jnp.dot(ckv_ref[...], wkvb_ref[...], preferred_element_type=F32)
    _assemble_kv(kv, kpe_ref[...], knw_ref[...], None, k_ref, v_ref)


def _ctx_kv(ckv, kpe, wkvb, knw, tm):
    n = ckv.shape[0]
    nq = MLA_HEADS * HEAD_PAD
    nv = MLA_HEADS * V_HEAD
    row = lambda w: pl.BlockSpec((tm, w), lambda i: (i, 0))
    return pl.pallas_call(
        _ctx_kv_body,
        grid=(n // tm,),
        in_specs=[row(KV_LORA), row(LANES), _const_spec((KV_LORA, nq)), _const_spec((1, HEAD_PAD))],
        out_specs=[row(nq), pl.BlockSpec((nv, tm), lambda i: (0, i))],
        out_shape=[jax.ShapeDtypeStruct((n, nq), BF16), jax.ShapeDtypeStruct((nv, n), BF16)],
        compiler_params=_cparams(("arbitrary",)),
        name="ctx_kv",
    )(ckv, kpe, wkvb, knw)


_NT = (((1,), (1,)), ((), ()))
_EXP2_SCALE = math.log2(math.e) * QK_HEAD ** -0.5

ATTN_SUB = 512
ATTN_KCHUNK = 1024


def _attn_sample_body(q_ref, kc_ref, kl_ref, vct_ref, vlt_ref, o_ref, s0_ref, s1_ref, vt_ref):
    past = kc_ref.shape[0]
    nk = kl_ref.shape[0]
    sub = s0_ref.shape[1]
    nsub = q_ref.shape[0] // sub

    vt_ref[0:V_HEAD, 0:past] = vct_ref[...]
    vt_ref[0:V_HEAD, past:] = vlt_ref[...]
    vt_ref[V_HEAD:, :] = jnp.ones((ONES_ROWS, past + nk), BF16)

    def rows(i):
        return pl.ds(pl.multiple_of(i * sub, sub), sub)

    def scores(i, s_ref):
        q = q_ref[rows(i), :]
        s_ref[0:past, :] = lax.dot_general(kc_ref[...], q, _NT, preferred_element_type=F32)
        s_ref[past:, :] = lax.dot_general(kl_ref[...], q, _NT, preferred_element_type=F32)

    kchunk = min(ATTN_KCHUNK, nk)
    bounds = [0] + list(range(past, past + nk + 1, kchunk))

    def finish(i, s_ref):
        m = jnp.max(s_ref[...], axis=0, keepdims=True)
        o = None
        for lo, hi in zip(bounds[:-1], bounds[1:]):
            p = jnp.exp2(s_ref[lo:hi, :] - m).astype(BF16)
            part = jnp.dot(vt_ref[:, lo:hi], p, preferred_element_type=F32)
            o = part if o is None else o + part
        o_ref[rows(i), :] = (o[:V_HEAD] / o[V_HEAD:V_HEAD + 1]).T.astype(BF16)

    scores(0, s0_ref)

    def pair(t, carry):
        i = 2 * t
        scores(i + 1, s1_ref)
        finish(i, s0_ref)
        scores(i + 2, s0_ref)
        finish(i + 1, s1_ref)
        return carry

    lax.fori_loop(0, nsub // 2 - 1, pair, 0)
    scores(nsub - 1, s1_ref)
    finish(nsub - 2, s0_ref)
    finish(nsub - 1, s1_ref)


def _attn_sample(g, q, k, vt, kc, vct, past):
    sub = min(ATTN_SUB, g.dec_s // 2)
    assert g.dec_s % (2 * sub) == 0 and g.dec_s % min(ATTN_KCHUNK, g.dec_s) == 0
    nkeys = past + g.dec_s
    score_buf = pltpu.VMEM((nkeys, sub), F32)
    return pl.pallas_call(
        _attn_sample_body,
        grid=(g.dec_b, MLA_HEADS),
        in_specs=[
            pl.BlockSpec((g.dec_s, HEAD_PAD), lambda b, h: (b, h)),
            pl.BlockSpec((past, HEAD_PAD), lambda b, h: (b, h)),
            pl.BlockSpec((g.dec_s, HEAD_PAD), lambda b, h: (b, h)),
            pl.BlockSpec((V_HEAD, past), lambda b, h: (h, b)),
            pl.BlockSpec((V_HEAD, g.dec_s), lambda b, h: (h, b)),
        ],
        out_specs=pl.BlockSpec((g.dec_s, V_HEAD), lambda b, h: (b, h)),
        out_shape=jax.ShapeDtypeStruct((g.ns, MLA_HEADS * V_HEAD), BF16),
        scratch_shapes=[score_buf, score_buf, pltpu.VMEM((V_HEAD + ONES_ROWS, nkeys), BF16)],
        compiler_params=_cparams(("arbitrary", "arbitrary")),
        name="attn_sample",
    )(q, kc, k, vct, vt)


def _attn_prompt_body(q_ref, k_ref, vt_ref, o_ref):
    for h in range(MLA_HEADS):
        qk = slice(h * HEAD_PAD, (h + 1) * HEAD_PAD)
        s = lax.dot_general(q_ref[:, qk], k_ref[:, qk], _NT, preferred_element_type=F32)
        m = jnp.max(s, axis=-1, keepdims=True)
        p = jnp.exp2(s - m).astype(BF16)
        l = jnp.sum(p.astype(F32), axis=-1, keepdims=True)
        o = lax.dot_general(p, vt_ref[h * V_HEAD:(h + 1) * V_HEAD, :], _NT, preferred_element_type=F32)
        o_ref[:, h * V_HEAD:(h + 1) * V_HEAD] = (o / l).astype(BF16)


def _attn_prompt(g, q, k, vt):
    off = g.ns // g.s
    return pl.pallas_call(
        _attn_prompt_body,
        grid=(g.b,),
        in_specs=[
            pl.BlockSpec((g.s, MLA_HEADS * HEAD_PAD), lambda b: (off + b, 0)),
            pl.BlockSpec((g.s, MLA_HEADS * HEAD_PAD), lambda b: (off + b, 0)),
            pl.BlockSpec((MLA_HEADS * V_HEAD, g.s), lambda b: (0, off + b)),
        ],
        out_specs=pl.BlockSpec((g.s, MLA_HEADS * V_HEAD), lambda b: (b, 0)),
        out_shape=jax.ShapeDtypeStruct((g.np_, MLA_HEADS * V_HEAD), BF16),
        compiler_params=_cparams(("arbitrary",)),
        name="attn_prompt",
    )(q, k, vt)


def _chunk_pos(g, ch):
    n = pl.program_id(0)
    n_s = g.ns // ch
    cps_s = g.dec_s // ch
    cps_p = g.s // ch
    is_s = n < n_s
    j = jnp.where(is_s, n % cps_s, (n - n_s) % cps_p)
    cps = jnp.where(is_s, cps_s, cps_p)
    return j, cps


def _halo_specs(g, ch, halo, width):
    per = ch // halo
    last = g.t // halo - 1
    main = pl.BlockSpec((ch, width), lambda n: (n, 0))
    left = pl.BlockSpec((halo, width), lambda n: (jnp.maximum(n * per - 1, 0), 0))
    right = pl.BlockSpec((halo, width), lambda n: (jnp.minimum((n + 1) * per, last), 0))
    return main, left, right


CONV_ROWS = 32


def _conv_body(g, ch, x_ref, xl_ref, xr_ref, w_ref, b_ref, lnw_ref, lnb_ref, o_ref, xp_ref, xs_ref):
    j, cps = _chunk_pos(g, ch)
    xp_ref[0:CONV_HALO] = jnp.where(j > 0, xl_ref[...], 0.0)
    xp_ref[CONV_HALO:CONV_HALO + ch] = x_ref[...]
    xp_ref[CONV_HALO + ch:2 * CONV_HALO + ch] = jnp.where(j < cps - 1, xr_ref[...], 0.0)
    bias = b_ref[...]
    lnw = lnw_ref[...]
    lnb = lnb_ref[...]
    base = CONV_HALO - CONV_K // 2

    span = xs_ref.shape[1]
    for ph in range(1, SUBLANES):
        xs_ref[ph] = xp_ref[ph:ph + span, :]

    def src(lo, ph):
        return xp_ref[lo:lo + SUBLANES, :] if ph == 0 else xs_ref[ph, lo:lo + SUBLANES, :]

    for r0 in range(0, ch, CONV_ROWS):
        accs = [None] * (CONV_ROWS // SUBLANES)
        for k in range(CONV_K):
            wk = w_ref[k]
            ph = (base + k) % SUBLANES
            for jj in range(len(accs)):
                term = src(r0 + jj * SUBLANES + base + k - ph, ph) * wk
                accs[jj] = term if accs[jj] is None else accs[jj] + term
        acc = jnp.concatenate(accs, axis=0) + bias
        mu = jnp.mean(acc, axis=-1, keepdims=True)
        xc = acc - mu
        var = jnp.mean(xc * xc, axis=-1, keepdims=True)
        y = xc * lax.rsqrt(var + EPS) * lnw + lnb
        o_ref[r0:r0 + CONV_ROWS, :] = _silu(y).astype(BF16)


def _conv(g, glu, w, b, lnw, lnb, ch):
    cc = glu.shape[1]
    main, left, right = _halo_specs(g, ch, CONV_HALO, cc)
    return pl.pallas_call(
        functools.partial(_conv_body, g, ch),
        grid=(g.t // ch,),
        in_specs=[main, left, right, _const_spec((CONV_K, SUBLANES, cc)), _const_spec((1, cc)),
                  _const_spec((1, cc)), _const_spec((1, cc))],
        out_specs=pl.BlockSpec((ch, cc), lambda n: (n, 0)),
        out_shape=jax.ShapeDtypeStruct((g.t, cc), BF16),
        scratch_shapes=[pltpu.VMEM((ch + 2 * CONV_HALO, cc), F32),
                        pltpu.VMEM((SUBLANES, ch + 2 * CONV_HALO - SUBLANES, cc), F32)],
        compiler_params=_cparams(("arbitrary",)),
        name="conv_module",
    )(glu, glu, glu, w, b, lnw, lnb)


def _mix_out_body(d, cc, n_s_tiles, nx, *refs):
    x_refs = refs[:nx]
    mod_ref, ca_ref, as_ref, ap_ref, wo_ref, o_ref = refs[nx:]
    i = pl.program_id(0)
    attn = jnp.where(i < n_s_tiles, as_ref[...], ap_ref[...])
    y = jnp.dot(ca_ref[...], wo_ref[0:cc, :], preferred_element_type=F32)
    y = y + jnp.dot(attn, wo_ref[cc:, :], preferred_element_type=F32)
    g1 = mod_ref[0][:, 2 * d:3 * d]
    o_ref[...] = _x_load(x_refs, i < n_s_tiles) + g1 * y


def _mix_out(g, layer, x, mod, cact, attn_s, attn_p, wo, tm):
    d = g.d
    cc = cact.shape[1]
    na = attn_s.shape[1]
    n_s_tiles = g.ns // tm
    mrow = g.mod_row(layer, tm)
    x_specs, xs = _x_in(g, tm, x)
    return pl.pallas_call(
        functools.partial(_mix_out_body, d, cc, n_s_tiles, len(xs)),
        grid=(g.t // tm,),
        in_specs=x_specs + [
            pl.BlockSpec((1, 1, N_MOD * d), lambda i: (mrow(i), 0, 0)),
            pl.BlockSpec((tm, cc), lambda i: (i, 0)),
            pl.BlockSpec((tm, na), lambda i: (jnp.minimum(i, n_s_tiles - 1), 0)),
            pl.BlockSpec((tm, na), lambda i: (jnp.maximum(i - n_s_tiles, 0), 0)),
            _const_spec((cc + na, d)),
        ],
        out_specs=pl.BlockSpec((tm, d), lambda i: (i, 0)),
        out_shape=jax.ShapeDtypeStruct((g.t, d), F32),
        compiler_params=_cparams(("arbitrary",)),
        name="even_out",
    )(*xs, mod, cact, attn_s, attn_p, wo)


def _pool_body(g, ch, x_ref, xl_ref, xr_ref, mod_ref, n1w_ref, pw_ref, ps_ref, o_ref, hb_ref, mx_ref,
               *lvl_refs):
    d = g.d
    pg = d // len(POOL_WINDOWS)
    j, cps = _chunk_pos(g, ch)
    m = mod_ref[0]
    n1w = n1w_ref[...]
    scale = 1.0 + m[:, d:2 * d]
    shift = m[:, 0:d]

    def norm(x):
        return _rms(x, n1w) * scale + shift

    pad = POOL_HALO
    base = pad + POOL_HALO
    total = base + ch + POOL_HALO
    hb_ref[0:pad] = jnp.zeros((pad, d), F32)
    hb_ref[pad:base] = jnp.where(j > 0, norm(xl_ref[...]), 0.0)
    hb_ref[base:base + ch] = norm(x_ref[...])
    hb_ref[base + ch:total] = jnp.where(j < cps - 1, norm(xr_ref[...]), 0.0)
    seq_len = cps * ch

    levels = (hb_ref,) + tuple(lvl_refs)
    for k in range(1, len(levels)):
        src, step = levels[k - 1], 1 << (k - 1)
        levels[k][0:pad] = jnp.zeros((pad, levels[k].shape[1]), F32)
        levels[k][pad:total] = src[pad:total, pg:] + src[pad - step:total - step, pg:]

    pos = j * ch + lax.broadcasted_iota(jnp.int32, (ch, 1), 0)
    for gi, w in enumerate(POOL_WINDOWS):
        half = w // 2
        assert half == 1 << gi
        cols = slice(gi * pg, (gi + 1) * pg)
        src = levels[gi]
        win = src[base - 1:base - 1 + ch, 0:pg] + src[base + half - 1:base + half - 1 + ch, 0:pg]
        cnt = jnp.minimum(pos + half, seq_len) - jnp.maximum(pos - half, 0)
        mx_ref[:, cols] = (win / cnt.astype(F32) - hb_ref[base:base + ch, cols]).astype(BF16)

    g1 = m[:, 2 * d:3 * d]
    ps = ps_ref[...]
    for gi in range(len(POOL_WINDOWS)):
        cols = slice(gi * pg, (gi + 1) * pg)
        y = jnp.dot(mx_ref[:, cols], pw_ref[gi], preferred_element_type=F32)
        o_ref[:, cols] = x_ref[:, cols] + g1[:, cols] * (y * ps[:, cols])


def _pool(g, layer, x, mod, n1w, pw, ps, ch):
    d = g.d
    pg = d // len(POOL_WINDOWS)
    main, left, right = _halo_specs(g, ch, POOL_HALO, d)
    mrow = g.mod_row(layer, ch)
    return pl.pallas_call(
        functools.partial(_pool_body, g, ch),
        grid=(g.t // ch,),
        in_specs=[main, left, right,
                  pl.BlockSpec((1, 1, N_MOD * d), lambda n: (mrow(n), 0, 0)),
                  _const_spec((1, d)),
                  _const_spec((len(POOL_WINDOWS), pg, pg)),
                  _const_spec((1, d))],
        out_specs=pl.BlockSpec((ch, d), lambda n: (n, 0)),
        out_shape=jax.ShapeDtypeStruct((g.t, d), F32),
        scratch_shapes=[pltpu.VMEM((ch + 3 * POOL_HALO, d), F32), pltpu.VMEM((ch, d), BF16)]
        + [pltpu.VMEM((ch + 3 * POOL_HALO, d - k * pg), F32) for k in range(1, len(POOL_WINDOWS))],
        compiler_params=_cparams(("arbitrary",)),
        name="pool_mixer",
    )(x, x, x, mod, n1w, pw, ps)


def _ffn_body(d, nj, x_ref, mod_ref, n2w_ref, wg_ref, wu_ref, wd_ref, o_ref, h_ref):
    j = pl.program_id(1)
    m = mod_ref[0]

    def gated_chunk(h):
        gate = jnp.dot(h, wg_ref[...], preferred_element_type=F32)
        up = jnp.dot(h, wu_ref[...], preferred_element_type=F32)
        a = (_silu(gate) * up).astype(BF16)
        return m[:, 5 * d:6 * d] * jnp.dot(a, wd_ref[...], preferred_element_type=F32)

    @pl.when(j == 0)
    def _():
        x = x_ref[...]
        h = (_rms(x, n2w_ref[...]) * (1.0 + m[:, 4 * d:5 * d]) + m[:, 3 * d:4 * d]).astype(BF16)
        h_ref[...] = h
        o_ref[...] = x + gated_chunk(h)

    @pl.when(j > 0)
    def _():
        o_ref[...] += gated_chunk(h_ref[...])


def _ffn(g, layer, x, mod, n2w, wg, wu, wd, tm, tf, tile0=0, ntiles=None):
    d = g.d
    dff = wg.shape[2]
    nj = dff // tf
    ntiles = g.t // tm if ntiles is None else ntiles
    mrow = g.mod_row(layer, tm)
    return pl.pallas_call(
        functools.partial(_ffn_body, d, nj),
        grid=(ntiles, nj),
        in_specs=[
            pl.BlockSpec((tm, d), lambda i, j: (tile0 + i, 0)),
            pl.BlockSpec((1, 1, N_MOD * d), lambda i, j: (mrow(tile0 + i), 0, 0)),
            pl.BlockSpec((1, d), lambda i, j: (0, 0)),
            pl.BlockSpec((None, d, tf), lambda i, j: (layer, 0, j)),
            pl.BlockSpec((None, d, tf), lambda i, j: (layer, 0, j)),
            pl.BlockSpec((None, tf, d), lambda i, j: (layer, j, 0)),
        ],
        out_specs=pl.BlockSpec((tm, d), lambda i, j: (i, 0)),
        out_shape=jax.ShapeDtypeStruct((ntiles * tm, d), F32),
        scratch_shapes=[pltpu.VMEM((tm, d), BF16)],
        compiler_params=_cparams(("arbitrary", "arbitrary")),
        name="ffn",
    )(x, mod, n2w, wg, wu, wd)


def _rope_tables(g, tm):
    rows = g.dec_s // GRID_W
    row = jnp.broadcast_to(jnp.arange(rows, dtype=F32)[:, None], (rows, GRID_W)).reshape(-1)
    col = jnp.broadcast_to(jnp.arange(GRID_W, dtype=F32)[None, :], (rows, GRID_W)).reshape(-1)
    n_freq = QK_ROPE // 4
    inv_freq = 1.0 / (ROPE_THETA ** (jnp.arange(n_freq, dtype=F32) / n_freq))
    ang = jnp.concatenate([row[:, None] * inv_freq, col[:, None] * inv_freq], axis=-1)
    cos, sin = jnp.cos(ang), jnp.sin(ang)
    z = jnp.zeros_like(cos)
    c = jnp.concatenate([cos, z, cos, z], axis=-1)
    s = jnp.concatenate([-sin, z, sin, z], axis=-1)
    c = jnp.concatenate([c, jnp.ones((tm, LANES), F32)], axis=0)
    s = jnp.concatenate([s, jnp.zeros((tm, LANES), F32)], axis=0)
    return c, s


def kernel(x_prompt, x_sample, cache_ckv, cache_kpe, c, c_ctx, norm1_w, norm2_w, w_mod, b_mod, w_in, conv_dw_w, conv_dw_b, conv_ln_w, conv_ln_b, q_a_norm_w, w_q_b, kv_a_norm_w, w_kv_b, q_norm_w, k_norm_w, w_out, pool_w, pool_scale, ffn_w_gate, ffn_w_up, ffn_w_down):
    b, s, d = x_prompt.shape
    dec_b, dec_s, _ = x_sample.shape
    past = cache_ckv.shape[2]
    depth = norm1_w.shape[0]
    g = _Geom(dec_b, dec_s, b, s, d)
    cc = d // 2
    assert dec_b + 1 <= MOD_ROWS and g.ns % s == 0 and dec_s % GRID_W == 0

    tm_proj = min(256, s)
    tm_out = min(512, g.np_, dec_s)
    tm_ffn = min(1024, g.np_, dec_s)
    tf = 512
    ch = s

    cond = jnp.concatenate([c, c_ctx[None, :], jnp.zeros((MOD_ROWS - dec_b - 1, d), F32)], axis=0)
    mod = _modulation(cond, w_mod, b_mod).reshape(depth * MOD_ROWS, 1, N_MOD * d)

    rope_c, rope_s = _rope_tables(g, tm_proj)
    x = (x_sample.reshape(g.ns, d), x_prompt.reshape(g.np_, d))
    wg_all = ffn_w_gate.astype(BF16)
    wu_all = ffn_w_up.astype(BF16)
    wd_all = ffn_w_down.astype(BF16)

    new_ckv, new_kpe = [], []
    for layer in range(depth):
        n1w = norm1_w[layer][None, :]
        if layer % 2 == 0:
            e = layer // 2
            win = jnp.concatenate([w_in[e][:, :2 * cc + Q_LORA + KV_LORA],
                                   _pad_rope(w_in[e][:, 2 * cc + Q_LORA + KV_LORA:])], axis=-1).astype(BF16)
            wq = w_q_b[e].reshape(Q_LORA, MLA_HEADS, QK_HEAD)
            wqb = jnp.concatenate([wq[..., :QK_NOPE], _pad_rope(wq[..., QK_NOPE:])], axis=-1)
            wqb = wqb.reshape(Q_LORA, MLA_HEADS * HEAD_PAD).astype(BF16)
            wkv = w_kv_b[e].reshape(KV_LORA, MLA_HEADS, QK_NOPE + V_HEAD)
            wkvb = jnp.concatenate([wkv[..., :QK_NOPE].reshape(KV_LORA, -1),
                                    wkv[..., QK_NOPE:].reshape(KV_LORA, -1)], axis=-1).astype(BF16)
            qnw = jnp.concatenate([q_norm_w[e][:QK_NOPE], _pad_rope(q_norm_w[e][QK_NOPE:])])[None, :]
            qnw = qnw * _EXP2_SCALE
            knw = jnp.concatenate([k_norm_w[e][:QK_NOPE], _pad_rope(k_norm_w[e][QK_NOPE:])])[None, :]

            glu, q, k, v, ckv, kpe = _proj(
                g, layer, x, mod, n1w, win, q_a_norm_w[e][None, :], wqb, kv_a_norm_w[e][None, :], wkvb,
                qnw, knw, rope_c, rope_s, tm_proj)
            new_ckv.append(ckv[g.ns:].reshape(b, s, KV_LORA))
            new_kpe.append(_unpad_rope(kpe[g.ns:]).reshape(b, s, QK_ROPE))

            kc, vc = _ctx_kv(cache_ckv[:, e].reshape(dec_b * past, KV_LORA).astype(BF16),
                             _pad_rope(cache_kpe[:, e]).reshape(dec_b * past, LANES), wkvb, knw, past)
            attn_s = _attn_sample(g, q, k, v, kc, vc, past)
            attn_p = _attn_prompt(g, q, k, v)
            w_taps = jnp.broadcast_to(conv_dw_w[e][:, None, :], (CONV_K, SUBLANES, cc))
            cact = _conv(g, glu, w_taps, conv_dw_b[e][None, :], conv_ln_w[e][None, :],
                         conv_ln_b[e][None, :], ch)
            x = _mix_out(g, layer, x, mod, cact, attn_s, attn_p, w_out[e].astype(BF16), tm_out)
        else:
            o = layer // 2
            x = _pool(g, layer, x, mod, n1w, pool_w[o].astype(BF16), pool_scale[o][None, :], ch)
        ffn = functools.partial(_ffn, g, layer, x, mod, norm2_w[layer][None, :], wg_all, wu_all, wd_all,
                                tm_ffn, tf)
        if layer < depth - 1:
            x = ffn()
        else:
            n_s_tiles = g.ns // tm_ffn
            x = (ffn(tile0=0, ntiles=n_s_tiles), ffn(tile0=n_s_tiles, ntiles=g.np_ // tm_ffn))

    y_sample = x[0].reshape(dec_b, dec_s, d)
    y_prompt = x[1].reshape(b, s, d)
    return (y_prompt, y_sample, jnp.stack(new_ckv, axis=1), jnp.stack(new_kpe, axis=1))
```

```python
import functools
import math

import jax
import jax.numpy as jnp
from jax import lax
from jax.experimental import pallas as pl
from jax.experimental.pallas import tpu as pltpu

F32 = jnp.float32
BF16 = jnp.bfloat16

EPS = 1e-6
GRID_W = 64
CONV_K = 31
MLA_HEADS = 8
QK_NOPE = 128
QK_ROPE = 64
V_HEAD = 128
QK_HEAD = QK_NOPE + QK_ROPE
Q_LORA = 768
KV_LORA = 512
ROPE_THETA = 10000.0
POOL_WINDOWS = (2, 4, 8, 16)
N_MOD = 6

LANES = 128
SUBLANES = 8
HEAD_PAD = 2 * LANES
ONES_ROWS = 16
ROPE_HALF = QK_ROPE // 2
MOD_ROWS = 16
CONV_HALO = 16
POOL_HALO = 8
VMEM_LIMIT = 62 * 1024 * 1024


def _cparams(sem, vmem=VMEM_LIMIT):
    return pltpu.CompilerParams(dimension_semantics=sem, vmem_limit_bytes=vmem)


def _const_spec(shape):
    nd = len(shape)
    return pl.BlockSpec(shape, lambda *_: (0,) * nd, pipeline_mode=pl.Buffered(1))


def _rms(x, w):
    ms = jnp.mean(x * x, axis=-1, keepdims=True)
    return x * lax.rsqrt(ms + EPS) * w


def _silu(x):
    return x * jax.nn.sigmoid(x)


def _pad_rope(w):
    z = jnp.zeros(w.shape[:-1] + (ROPE_HALF,), w.dtype)
    return jnp.concatenate([w[..., 0::2], z, w[..., 1::2], z], axis=-1)


def _unpad_rope(w):
    return jnp.stack([w[..., 0:ROPE_HALF], w[..., 2 * ROPE_HALF:3 * ROPE_HALF]], axis=-1).reshape(
        w.shape[:-1] + (QK_ROPE,))


class _Geom:
    def __init__(self, dec_b, dec_s, b, s, d):
        self.dec_b, self.dec_s, self.b, self.s, self.d = dec_b, dec_s, b, s, d
        self.ns = dec_b * dec_s
        self.np_ = b * s
        self.t = self.ns + self.np_

    def mod_row(self, layer, tile):
        def f(i):
            tok = i * tile
            return layer * MOD_ROWS + jnp.where(tok < self.ns, tok // self.dec_s, self.dec_b)
        return f


def _x_in(g, tm, x, idx=lambda i: i):
    d = g.d
    if not isinstance(x, tuple):
        return [pl.BlockSpec((tm, d), lambda *a: (idx(*a), 0))], (x,)
    n_s = g.ns // tm
    return [pl.BlockSpec((tm, d), lambda *a: (jnp.minimum(idx(*a), n_s - 1), 0)),
            pl.BlockSpec((tm, d), lambda *a: (jnp.maximum(idx(*a) - n_s, 0), 0))], x


def _x_load(x_refs, is_sample):
    if len(x_refs) == 1:
        return x_refs[0][...]
    return jnp.where(is_sample, x_refs[0][...], x_refs[1][...])


def _mod_body(c_ref, w_ref, b_ref, o_ref):
    s = _silu(c_ref[...]).astype(BF16)
    w = w_ref[0].astype(BF16)
    o_ref[0] = jnp.dot(s, w, preferred_element_type=F32) + b_ref[0]


def _modulation(cond, w_mod, b_mod):
    depth, d, n = w_mod.shape
    tn = 1024
    return pl.pallas_call(
        _mod_body,
        grid=(depth, n // tn),
        in_specs=[
            pl.BlockSpec((MOD_ROWS, d), lambda l, j: (0, 0)),
            pl.BlockSpec((1, d, tn), lambda l, j: (l, 0, j)),
            pl.BlockSpec((1, 1, tn), lambda l, j: (l, 0, j)),
        ],
        out_specs=pl.BlockSpec((1, MOD_ROWS, tn), lambda l, j: (l, 0, j)),
        out_shape=jax.ShapeDtypeStruct((depth, MOD_ROWS, n), F32),
        compiler_params=_cparams(("arbitrary", "arbitrary")),
        name="modulation",
    )(cond, w_mod, b_mod.reshape(depth, 1, n))


def _rope128(x, c, s):
    return x * c + pltpu.roll(x, 2 * ROPE_HALF, 1) * s


def _assemble_kv(kv, kpe, knw, rope, k_ref, vt_ref):
    hv = MLA_HEADS * QK_NOPE
    kpe_ss = jnp.sum(kpe * kpe, axis=-1, keepdims=True)
    kpe_w = kpe * knw[:, QK_NOPE:]
    if rope is not None:
        kpe_w = _rope128(kpe_w, rope[0], rope[1])
    for h in range(MLA_HEADS):
        kn = kv[:, h * QK_NOPE:(h + 1) * QK_NOPE]
        ss = jnp.sum(kn * kn, axis=-1, keepdims=True) + kpe_ss
        r = lax.rsqrt(ss * (1.0 / QK_HEAD) + EPS)
        k_ref[:, h * HEAD_PAD:h * HEAD_PAD + QK_NOPE] = (kn * r * knw[:, :QK_NOPE]).astype(BF16)
        k_ref[:, h * HEAD_PAD + QK_NOPE:(h + 1) * HEAD_PAD] = (kpe_w * r).astype(BF16)
    vt_ref[...] = kv[:, hv:].T.astype(BF16)


def _proj_body(d, cc, n_s_tiles, nx, *refs):
    x_refs = refs[:nx]
    (mod_ref, n1w_ref, win_ref, qanw_ref, wqb_ref, kvanw_ref, wkvb_ref, qnw_ref, knw_ref, rc_ref, rs_ref,
     glu_ref, q_ref, k_ref, v_ref, ckv_ref, kpe_ref) = refs[nx:]
    m = mod_ref[0]
    x = _x_load(x_refs, pl.program_id(0) < n_s_tiles)
    h = _rms(x, n1w_ref[...]) * (1.0 + m[:, d:2 * d]) + m[:, 0:d]
    proj = jnp.dot(h.astype(BF16), win_ref[...], preferred_element_type=F32)
    o1 = 2 * cc
    o2 = o1 + Q_LORA
    o3 = o2 + KV_LORA
    glu_ref[...] = proj[:, :cc] * jax.nn.sigmoid(proj[:, cc:o1])

    rc = rc_ref[...]
    rs = rs_ref[...]
    qa = _rms(proj[:, o1:o2], qanw_ref[...]).astype(BF16)
    qb = jnp.dot(qa, wqb_ref[...], preferred_element_type=F32)
    qnw = qnw_ref[...]
    for hd in range(MLA_HEADS):
        qh = qb[:, hd * HEAD_PAD:(hd + 1) * HEAD_PAD]
        ss = jnp.sum(qh * qh, axis=-1, keepdims=True)
        r = lax.rsqrt(ss * (1.0 / QK_HEAD) + EPS)
        qn = qh * r * qnw
        q_ref[:, hd * HEAD_PAD:hd * HEAD_PAD + QK_NOPE] = qn[:, :QK_NOPE].astype(BF16)
        q_ref[:, hd * HEAD_PAD + QK_NOPE:(hd + 1) * HEAD_PAD] = _rope128(qn[:, QK_NOPE:], rc, rs).astype(BF16)

    ckv = _rms(proj[:, o2:o3], kvanw_ref[...])
    ckv_ref[...] = ckv
    kpe = proj[:, o3:]
    kpe_ref[...] = kpe
    kv = jnp.dot(ckv.astype(BF16), wkvb_ref[...], preferred_element_type=F32)
    _assemble_kv(kv, kpe, knw_ref[...], (rc, rs), k_ref, v_ref)


def _proj(g, layer, x, mod, n1w, win, qanw, wqb, kvanw, wkvb, qnw, knw, rope_c, rope_s, tm):
    d = g.d
    cc = d // 2
    nin = win.shape[1]
    nq = MLA_HEADS * HEAD_PAD
    nv = MLA_HEADS * V_HEAD
    s_tiles = g.dec_s // tm

    def rope_idx(i):
        return (jnp.where(i * tm < g.ns, i % s_tiles, s_tiles), 0)

    row = lambda n: pl.BlockSpec((tm, n), lambda i: (i, 0))
    mrow = g.mod_row(layer, tm)
    x_specs, xs = _x_in(g, tm, x)
    return pl.pallas_call(
        functools.partial(_proj_body, d, cc, g.ns // tm, len(xs)),
        grid=(g.t // tm,),
        in_specs=x_specs + [
            pl.BlockSpec((1, 1, N_MOD * d), lambda i: (mrow(i), 0, 0)),
            _const_spec((1, d)),
            _const_spec((d, nin)),
            _const_spec((1, Q_LORA)),
            _const_spec((Q_LORA, nq)),
            _const_spec((1, KV_LORA)),
            _const_spec((KV_LORA, nq)),
            _const_spec((1, HEAD_PAD)),
            _const_spec((1, HEAD_PAD)),
            pl.BlockSpec((tm, LANES), rope_idx),
            pl.BlockSpec((tm, LANES), rope_idx),
        ],
        out_specs=[row(cc), row(nq), row(nq), pl.BlockSpec((nv, tm), lambda i: (0, i)), row(KV_LORA),
                   row(LANES)],
        out_shape=[
            jax.ShapeDtypeStruct((g.t, cc), F32),
            jax.ShapeDtypeStruct((g.t, nq), BF16),
            jax.ShapeDtypeStruct((g.t, nq), BF16),
            jax.ShapeDtypeStruct((nv, g.t), BF16),
            jax.ShapeDtypeStruct((g.t, KV_LORA), F32),
            jax.ShapeDtypeStruct((g.t, LANES), F32),
        ],
        compiler_params=_cparams(("arbitrary",)),
        name="even_proj",
    )(*xs, mod, n1w, win, qanw, wqb, kvanw, wkvb, qnw, knw, rope_c, rope_s)


def _ctx_kv_body(ckv_ref, kpe_ref, wkvb_ref, knw_ref, k_ref, v_ref):
    kv = jnp.dot(ckv_ref[...], wkvb_ref[...], preferred_element_type=F32)
    _assemble_kv(kv, kpe_ref[...], knw_ref[...], None, k_ref, v_ref)


def _ctx_kv(ckv, kpe, wkvb, knw, tm):
    n = ckv.shape[0]
    nq = MLA_HEADS * HEAD_PAD
    nv = MLA_HEADS * V_HEAD
    row = lambda w: pl.BlockSpec((tm, w), lambda i: (i, 0))
    return pl.pallas_call(
        _ctx_kv_body,
        grid=(n // tm,),
        in_specs=[row(KV_LORA), row(LANES), _const_spec((KV_LORA, nq)), _const_spec((1, HEAD_PAD))],
        out_specs=[row(nq), pl.BlockSpec((nv, tm), lambda i: (0, i))],
        out_shape=[jax.ShapeDtypeStruct((n, nq), BF16), jax.ShapeDtypeStruct((nv, n), BF16)],
        compiler_params=_cparams(("arbitrary",)),
        name="ctx_kv",
    )(ckv, kpe, wkvb, knw)


_NT = (((1,), (1,)), ((), ()))
_EXP2_SCALE = math.log2(math.e) * QK_HEAD ** -0.5

ATTN_SUB = 512
ATTN_KCHUNK = 1024


def _attn_sample_body(q_ref, kc_ref, kl_ref, vct_ref, vlt_ref, o_ref, s0_ref, s1_ref, vt_ref):
    past = kc_ref.shape[0]
    nk = kl_ref.shape[0]
    sub = s0_ref.shape[1]
    nsub = q_ref.shape[0] // sub

    vt_ref[0:V_HEAD, 0:past] = vct_ref[...]
    vt_ref[0:V_HEAD, past:] = vlt_ref[...]
    vt_ref[V_HEAD:, :] = jnp.ones((ONES_ROWS, past + nk), BF16)

    def rows(i):
        return pl.ds(pl.multiple_of(i * sub, sub), sub)

    def scores(i, s_ref):
        q = q_ref[rows(i), :]
        s_ref[0:past, :] = lax.dot_general(kc_ref[...], q, _NT, preferred_element_type=F32)
        s_ref[past:, :] = lax.dot_general(kl_ref[...], q, _NT, preferred_element_type=F32)

    kchunk = min(ATTN_KCHUNK, nk)
    bounds = [0] + list(range(past, past + nk + 1, kchunk))

    def finish(i, s_ref):
        m = jnp.max(s_ref[...], axis=0, keepdims=True)
        o = None
        for lo, hi in zip(bounds[:-1], bounds[1:]):
            p = jnp.exp2(s_ref[lo:hi, :] - m).astype(BF16)
            part = jnp.dot(vt_ref[:, lo:hi], p, preferred_element_type=F32)
            o = part if o is None else o + part
        o_ref[rows(i), :] = (o[:V_HEAD] / o[V_HEAD:V_HEAD + 1]).T.astype(BF16)

    scores(0, s0_ref)

    def pair(t, carry):
        i = 2 * t
        scores(i + 1, s1_ref)
        finish(i, s0_ref)
        scores(i + 2, s0_ref)
        finish(i + 1, s1_ref)
        return carry

    lax.fori_loop(0, nsub // 2 - 1, pair, 0)
    scores(nsub - 1, s1_ref)
    finish(nsub - 2, s0_ref)
    finish(nsub - 1, s1_ref)


def _attn_sample(g, q, k, vt, kc, vct, past):
    sub = min(ATTN_SUB, g.dec_s // 2)
    assert g.dec_s % (2 * sub) == 0 and g.dec_s % min(ATTN_KCHUNK, g.dec_s) == 0
    nkeys = past + g.dec_s
    score_buf = pltpu.VMEM((nkeys, sub), F32)
    return pl.pallas_call(
        _attn_sample_body,
        grid=(g.dec_b, MLA_HEADS),
        in_specs=[
            pl.BlockSpec((g.dec_s, HEAD_PAD), lambda b, h: (b, h)),
            pl.BlockSpec((past, HEAD_PAD), lambda b, h: (b, h)),
            pl.BlockSpec((g.dec_s, HEAD_PAD), lambda b, h: (b, h)),
            pl.BlockSpec((V_HEAD, past), lambda b, h: (h, b)),
            pl.BlockSpec((V_HEAD, g.dec_s), lambda b, h: (h, b)),
        ],
        out_specs=pl.BlockSpec((g.dec_s, V_HEAD), lambda b, h: (b, h)),
        out_shape=jax.ShapeDtypeStruct((g.ns, MLA_HEADS * V_HEAD), BF16),
        scratch_shapes=[score_buf, score_buf, pltpu.VMEM((V_HEAD + ONES_ROWS, nkeys), BF16)],
        compiler_params=_cparams(("arbitrary", "arbitrary")),
        name="attn_sample",
    )(q, kc, k, vct, vt)


def _attn_prompt_body(q_ref, k_ref, vt_ref, o_ref):
    for h in range(MLA_HEADS):
        qk = slice(h * HEAD_PAD, (h + 1) * HEAD_PAD)
        s = lax.dot_general(q_ref[:, qk], k_ref[:, qk], _NT, preferred_element_type=F32)
        m = jnp.max(s, axis=-1, keepdims=True)
        p = jnp.exp2(s - m).astype(BF16)
        l = jnp.sum(p.astype(F32), axis=-1, keepdims=True)
        o = lax.dot_general(p, vt_ref[h * V_HEAD:(h + 1) * V_HEAD, :], _NT, preferred_element_type=F32)
        o_ref[:, h * V_HEAD:(h + 1) * V_HEAD] = (o / l).astype(BF16)


def _attn_prompt(g, q, k, vt):
    off = g.ns // g.s
    return pl.pallas_call(
        _attn_prompt_body,
        grid=(g.b,),
        in_specs=[
            pl.BlockSpec((g.s, MLA_HEADS * HEAD_PAD), lambda b: (off + b, 0)),
            pl.BlockSpec((g.s, MLA_HEADS * HEAD_PAD), lambda b: (off + b, 0)),
            pl.BlockSpec((MLA_HEADS * V_HEAD, g.s), lambda b: (0, off + b)),
        ],
        out_specs=pl.BlockSpec((g.s, MLA_HEADS * V_HEAD), lambda b: (b, 0)),
        out_shape=jax.ShapeDtypeStruct((g.np_, MLA_HEADS * V_HEAD), BF16),
        compiler_params=_cparams(("arbitrary",)),
        name="attn_prompt",
    )(q, k, vt)


def _chunk_pos(g, ch):
    n = pl.program_id(0)
    n_s = g.ns // ch
    cps_s = g.dec_s // ch
    cps_p = g.s // ch
    is_s = n < n_s
    j = jnp.where(is_s, n % cps_s, (n - n_s) % cps_p)
    cps = jnp.where(is_s, cps_s, cps_p)
    return j, cps


def _halo_specs(g, ch, halo, width):
    per = ch // halo
    last = g.t // halo - 1
    main = pl.BlockSpec((ch, width), lambda n: (n, 0))
    left = pl.BlockSpec((halo, width), lambda n: (jnp.maximum(n * per - 1, 0), 0))
    right = pl.BlockSpec((halo, width), lambda n: (jnp.minimum((n + 1) * per, last), 0))
    return main, left, right


CONV_ROWS = 32


def _conv_out_body(g, ch, nx, *refs):
    x_refs = refs[:nx]
    (glu_ref, gl_ref, gr_ref, w_ref, b_ref, lnw_ref, lnb_ref, mod_ref, as_ref, ap_ref, wo_ref,
     o_ref, xp_ref, xs_ref, ca_ref) = refs[nx:]
    d = g.d
    cc = glu_ref.shape[1]
    is_sample = pl.program_id(0) < g.ns // ch
    g1 = mod_ref[0][:, 2 * d:3 * d]

    attn = jnp.where(is_sample, as_ref[...], ap_ref[...])
    o_ref[...] = _x_load(x_refs, is_sample) + g1 * jnp.dot(attn, wo_ref[cc:, :], preferred_element_type=F32)

    j, cps = _chunk_pos(g, ch)
    xp_ref[0:CONV_HALO] = jnp.where(j > 0, gl_ref[...], 0.0)
    xp_ref[CONV_HALO:CONV_HALO + ch] = glu_ref[...]
    xp_ref[CONV_HALO + ch:2 * CONV_HALO + ch] = jnp.where(j < cps - 1, gr_ref[...], 0.0)
    bias = b_ref[...]
    lnw = lnw_ref[...]
    lnb = lnb_ref[...]
    base = CONV_HALO - CONV_K // 2

    span = xs_ref.shape[1]
    for ph in range(1, SUBLANES):
        xs_ref[ph] = xp_ref[ph:ph + span, :]

    def src(lo, ph):
        return xp_ref[lo:lo + SUBLANES, :] if ph == 0 else xs_ref[ph, lo:lo + SUBLANES, :]

    for r0 in range(0, ch, CONV_ROWS):
        accs = [None] * (CONV_ROWS // SUBLANES)
        for k in range(CONV_K):
            wk = w_ref[k]
            ph = (base + k) % SUBLANES
            for jj in range(len(accs)):
                term = src(r0 + jj * SUBLANES + base + k - ph, ph) * wk
                accs[jj] = term if accs[jj] is None else accs[jj] + term
        acc = jnp.concatenate(accs, axis=0) + bias
        mu = jnp.mean(acc, axis=-1, keepdims=True)
        xc = acc - mu
        var = jnp.mean(xc * xc, axis=-1, keepdims=True)
        y = xc * lax.rsqrt(var + EPS) * lnw + lnb
        ca_ref[r0:r0 + CONV_ROWS, :] = _silu(y).astype(BF16)

    o_ref[...] += g1 * jnp.dot(ca_ref[...], wo_ref[0:cc, :], preferred_element_type=F32)


def _conv_out(g, layer, x, mod, glu, w, b, lnw, lnb, attn_s, attn_p, wo, ch):
    d = g.d
    cc = glu.shape[1]
    na = attn_s.shape[1]
    n_s = g.ns // ch
    main, left, right = _halo_specs(g, ch, CONV_HALO, cc)
    mrow = g.mod_row(layer, ch)
    x_specs, xs = _x_in(g, ch, x)
    return pl.pallas_call(
        functools.partial(_conv_out_body, g, ch, len(xs)),
        grid=(g.t // ch,),
        in_specs=x_specs + [
            main, left, right, _const_spec((CONV_K, SUBLANES, cc)), _const_spec((1, cc)),
            _const_spec((1, cc)), _const_spec((1, cc)),
            pl.BlockSpec((1, 1, N_MOD * d), lambda n: (mrow(n), 0, 0)),
            pl.BlockSpec((ch, na), lambda n: (jnp.minimum(n, n_s - 1), 0)),
            pl.BlockSpec((ch, na), lambda n: (jnp.maximum(n - n_s, 0), 0)),
            _const_spec((cc + na, d)),
        ],
        out_specs=pl.BlockSpec((ch, d), lambda n: (n, 0)),
        out_shape=jax.ShapeDtypeStruct((g.t, d), F32),
        scratch_shapes=[pltpu.VMEM((ch + 2 * CONV_HALO, cc), F32),
                        pltpu.VMEM((SUBLANES, ch + 2 * CONV_HALO - SUBLANES, cc), F32),
                        pltpu.VMEM((ch, cc), BF16)],
        compiler_params=_cparams(("arbitrary",)),
        name="conv_out",
    )(*xs, glu, glu, glu, w, b, lnw, lnb, mod, attn_s, attn_p, wo)


def _pool_body(g, ch, x_ref, xl_ref, xr_ref, mod_ref, n1w_ref, pw_ref, ps_ref, o_ref, hb_ref, mx_ref,
               *lvl_refs):
    d = g.d
    pg = d // len(POOL_WINDOWS)
    j, cps = _chunk_pos(g, ch)
    m = mod_ref[0]
    w = n1w_ref[...] * (1.0 + m[:, d:2 * d])
    shift = m[:, 0:d]

    def norm(x):
        return _rms(x, w) + shift

    pad = POOL_HALO
    base = pad + POOL_HALO
    total = base + ch + POOL_HALO
    hb_ref[0:pad] = jnp.zeros((pad, d), F32)
    hb_ref[pad:base] = jnp.where(j > 0, norm(xl_ref[...]), 0.0)
    hb_ref[base:base + ch] = norm(x_ref[...])
    hb_ref[base + ch:total] = jnp.where(j < cps - 1, norm(xr_ref[...]), 0.0)
    seq_len = cps * ch

    levels = (hb_ref,) + tuple(lvl_refs)
    for k in range(1, len(levels)):
        src, step = levels[k - 1], 1 << (k - 1)
        levels[k][0:pad] = jnp.zeros((pad, levels[k].shape[1]), F32)
        levels[k][pad:total] = src[pad:total, pg:] + src[pad - step:total - step, pg:]

    pos = j * ch + lax.broadcasted_iota(jnp.int32, (ch, 1), 0)
    for gi, w in enumerate(POOL_WINDOWS):
        half = w // 2
        assert half == 1 << gi
        cols = slice(gi * pg, (gi + 1) * pg)
        src = levels[gi]
        win = src[base - 1:base - 1 + ch, 0:pg] + src[base + half - 1:base + half - 1 + ch, 0:pg]
        cnt = jnp.minimum(pos + half, seq_len) - jnp.maximum(pos - half, 0)
        mx_ref[:, cols] = (win / cnt.astype(F32) - hb_ref[base:base + ch, cols]).astype(BF16)

    g1 = m[:, 2 * d:3 * d]
    ps = ps_ref[...]
    for gi in range(len(POOL_WINDOWS)):
        cols = slice(gi * pg, (gi + 1) * pg)
        y = jnp.dot(mx_ref[:, cols], pw_ref[gi], preferred_element_type=F32)
        o_ref[:, cols] = x_ref[:, cols] + g1[:, cols] * (y * ps[:, cols])


def _pool(g, layer, x, mod, n1w, pw, ps, ch):
    d = g.d
    pg = d // len(POOL_WINDOWS)
    main, left, right = _halo_specs(g, ch, POOL_HALO, d)
    mrow = g.mod_row(layer, ch)
    return pl.pallas_call(
        functools.partial(_pool_body, g, ch),
        grid=(g.t // ch,),
        in_specs=[main, left, right,
                  pl.BlockSpec((1, 1, N_MOD * d), lambda n: (mrow(n), 0, 0)),
                  _const_spec((1, d)),
                  _const_spec((len(POOL_WINDOWS), pg, pg)),
                  _const_spec((1, d))],
        out_specs=pl.BlockSpec((ch, d), lambda n: (n, 0)),
        out_shape=jax.ShapeDtypeStruct((g.t, d), F32),
        scratch_shapes=[pltpu.VMEM((ch + 3 * POOL_HALO, d), F32), pltpu.VMEM((ch, d), BF16)]
        + [pltpu.VMEM((ch + 3 * POOL_HALO, d - k * pg), F32) for k in range(1, len(POOL_WINDOWS))],
        compiler_params=_cparams(("arbitrary",)),
        name="pool_mixer",
    )(x, x, x, mod, n1w, pw, ps)


def _ffn_body(d, nj, x_ref, mod_ref, n2w_ref, wg_ref, wu_ref, wd_ref, o_ref, h_ref):
    j = pl.program_id(1)
    m = mod_ref[0]

    def gated_chunk(h):
        gate = jnp.dot(h, wg_ref[...], preferred_element_type=F32)
        up = jnp.dot(h, wu_ref[...], preferred_element_type=F32)
        a = (_silu(gate) * up).astype(BF16)
        return m[:, 5 * d:6 * d] * jnp.dot(a, wd_ref[...], preferred_element_type=F32)

    @pl.when(j == 0)
    def _():
        x = x_ref[...]
        h = (_rms(x, n2w_ref[...]) * (1.0 + m[:, 4 * d:5 * d]) + m[:, 3 * d:4 * d]).astype(BF16)
        h_ref[...] = h
        o_ref[...] = x + gated_chunk(h)

    @pl.when(j > 0)
    def _():
        o_ref[...] += gated_chunk(h_ref[...])


def _ffn(g, layer, x, mod, n2w, wg, wu, wd, tm, tf, tile0=0, ntiles=None):
    d = g.d
    dff = wg.shape[2]
    nj = dff // tf
    ntiles = g.t // tm if ntiles is None else ntiles
    mrow = g.mod_row(layer, tm)
    return pl.pallas_call(
        functools.partial(_ffn_body, d, nj),
        grid=(ntiles, nj),
        in_specs=[
            pl.BlockSpec((tm, d), lambda i, j: (tile0 + i, 0)),
            pl.BlockSpec((1, 1, N_MOD * d), lambda i, j: (mrow(tile0 + i), 0, 0)),
            pl.BlockSpec((1, d), lambda i, j: (0, 0)),
            pl.BlockSpec((None, d, tf), lambda i, j: (layer, 0, j)),
            pl.BlockSpec((None, d, tf), lambda i, j: (layer, 0, j)),
            pl.BlockSpec((None, tf, d), lambda i, j: (layer, j, 0)),
        ],
        out_specs=pl.BlockSpec((tm, d), lambda i, j: (i, 0)),
        out_shape=jax.ShapeDtypeStruct((ntiles * tm, d), F32),
        scratch_shapes=[pltpu.VMEM((tm, d), BF16)],
        compiler_params=_cparams(("arbitrary", "arbitrary")),
        name="ffn",
    )(x, mod, n2w, wg, wu, wd)


def _rope_tables(g, tm):
    rows = g.dec_s // GRID_W
    row = jnp.broadcast_to(jnp.arange(rows, dtype=F32)[:, None], (rows, GRID_W)).reshape(-1)
    col = jnp.broadcast_to(jnp.arange(GRID_W, dtype=F32)[None, :], (rows, GRID_W)).reshape(-1)
    n_freq = QK_ROPE // 4
    inv_freq = 1.0 / (ROPE_THETA ** (jnp.arange(n_freq, dtype=F32) / n_freq))
    ang = jnp.concatenate([row[:, None] * inv_freq, col[:, None] * inv_freq], axis=-1)
    cos, sin = jnp.cos(ang), jnp.sin(ang)
    z = jnp.zeros_like(cos)
    c = jnp.concatenate([cos, z, cos, z], axis=-1)
    s = jnp.concatenate([-sin, z, sin, z], axis=-1)
    c = jnp.concatenate([c, jnp.ones((tm, LANES), F32)], axis=0)
    s = jnp.concatenate([s, jnp.zeros((tm, LANES), F32)], axis=0)
    return c, s


def kernel(x_prompt, x_sample, cache_ckv, cache_kpe, c, c_ctx, norm1_w, norm2_w, w_mod, b_mod, w_in, conv_dw_w, conv_dw_b, conv_ln_w, conv_ln_b, q_a_norm_w, w_q_b, kv_a_norm_w, w_kv_b, q_norm_w, k_norm_w, w_out, pool_w, pool_scale, ffn_w_gate, ffn_w_up, ffn_w_down):
    b, s, d = x_prompt.shape
    dec_b, dec_s, _ = x_sample.shape
    past = cache_ckv.shape[2]
    depth = norm1_w.shape[0]
    g = _Geom(dec_b, dec_s, b, s, d)
    cc = d // 2
    assert dec_b + 1 <= MOD_ROWS and g.ns % s == 0 and dec_s % GRID_W == 0

    tm_proj = min(256, s)
    tm_ffn = min(1024, g.np_, dec_s)
    tf = 512
    ch = s

    cond = jnp.concatenate([c, c_ctx[None, :], jnp.zeros((MOD_ROWS - dec_b - 1, d), F32)], axis=0)
    mod = _modulation(cond, w_mod, b_mod).reshape(depth * MOD_ROWS, 1, N_MOD * d)

    rope_c, rope_s = _rope_tables(g, tm_proj)
    x = (x_sample.reshape(g.ns, d), x_prompt.reshape(g.np_, d))
    wg_all = ffn_w_gate.astype(BF16)
    wu_all = ffn_w_up.astype(BF16)
    wd_all = ffn_w_down.astype(BF16)

    new_ckv, new_kpe = [], []
    for layer in range(depth):
        n1w = norm1_w[layer][None, :]
        if layer % 2 == 0:
            e = layer // 2
            win = jnp.concatenate([w_in[e][:, :2 * cc + Q_LORA + KV_LORA],
                                   _pad_rope(w_in[e][:, 2 * cc + Q_LORA + KV_LORA:])], axis=-1).astype(BF16)
            wq = w_q_b[e].reshape(Q_LORA, MLA_HEADS, QK_HEAD)
            wqb = jnp.concatenate([wq[..., :QK_NOPE], _pad_rope(wq[..., QK_NOPE:])], axis=-1)
            wqb = wqb.reshape(Q_LORA, MLA_HEADS * HEAD_PAD).astype(BF16)
            wkv = w_kv_b[e].reshape(KV_LORA, MLA_HEADS, QK_NOPE + V_HEAD)
            wkvb = jnp.concatenate([wkv[..., :QK_NOPE].reshape(KV_LORA, -1),
                                    wkv[..., QK_NOPE:].reshape(KV_LORA, -1)], axis=-1).astype(BF16)
            qnw = jnp.concatenate([q_norm_w[e][:QK_NOPE], _pad_rope(q_norm_w[e][QK_NOPE:])])[None, :]
            qnw = qnw * _EXP2_SCALE
            knw = jnp.concatenate([k_norm_w[e][:QK_NOPE], _pad_rope(k_norm_w[e][QK_NOPE:])])[None, :]

            glu, q, k, v, ckv, kpe = _proj(
                g, layer, x, mod, n1w, win, q_a_norm_w[e][None, :], wqb, kv_a_norm_w[e][None, :], wkvb,
                qnw, knw, rope_c, rope_s, tm_proj)
            new_ckv.append(ckv[g.ns:].reshape(b, s, KV_LORA))
            new_kpe.append(_unpad_rope(kpe[g.ns:]).reshape(b, s, QK_ROPE))

            kc, vc = _ctx_kv(cache_ckv[:, e].reshape(dec_b * past, KV_LORA).astype(BF16),
                             _pad_rope(cache_kpe[:, e]).reshape(dec_b * past, LANES), wkvb, knw, past)
            attn_s = _attn_sample(g, q, k, v, kc, vc, past)
            attn_p = _attn_prompt(g, q, k, v)
            w_taps = jnp.broadcast_to(conv_dw_w[e][:, None, :], (CONV_K, SUBLANES, cc))
            x = _conv_out(g, layer, x, mod, glu, w_taps, conv_dw_b[e][None, :], conv_ln_w[e][None, :],
                          conv_ln_b[e][None, :], attn_s, attn_p, w_out[e].astype(BF16), ch)
        else:
            o = layer // 2
            x = _pool(g, layer, x, mod, n1w, pool_w[o].astype(BF16), pool_scale[o][None, :], ch)
        ffn = functools.partial(_ffn, g, layer, x, mod, norm2_w[layer][None, :], wg_all, wu_all, wd_all,
                                tm_ffn, tf)
        if layer < depth - 1:
            x = ffn()
        else:
            n_s_tiles = g.ns // tm_ffn
            x = (ffn(tile0=0, ntiles=n_s_tiles), ffn(tile0=n_s_tiles, ntiles=g.np_ // tm_ffn))

    y_sample = x[0].reshape(dec_b, dec_s, d)
    y_prompt = x[1].reshape(b, s, d)
    return (y_prompt, y_sample, jnp.stack(new_ckv, axis=1), jnp.stack(new_kpe, axis=1))
```

```python
import functools
import math

import jax
import jax.numpy as jnp
from jax import lax
from jax.experimental import pallas as pl
from jax.experimental.pallas import tpu as pltpu

F32 = jnp.float32
BF16 = jnp.bfloat16

EPS = 1e-6
GRID_W = 64
CONV_K = 31
MLA_HEADS = 8
QK_NOPE = 128
QK_ROPE = 64
V_HEAD = 128
QK_HEAD = QK_NOPE + QK_ROPE
Q_LORA = 768
KV_LORA = 512
ROPE_THETA = 10000.0
POOL_WINDOWS = (2, 4, 8, 16)
N_MOD = 6

LANES = 128
SUBLANES = 8
HEAD_PAD = 2 * LANES
ONES_ROWS = 16
ROPE_HALF = QK_ROPE // 2
MOD_ROWS = 16
CONV_HALO = 16
POOL_HALO = 8
VMEM_LIMIT = 62 * 1024 * 1024


def _cparams(sem, vmem=VMEM_LIMIT):
    return pltpu.CompilerParams(dimension_semantics=sem, vmem_limit_bytes=vmem)


def _const_spec(shape):
    nd = len(shape)
    return pl.BlockSpec(shape, lambda *_: (0,) * nd, pipeline_mode=pl.Buffered(1))


def _rms(x, w):
    ms = jnp.mean(x * x, axis=-1, keepdims=True)
    return x * lax.rsqrt(ms + EPS) * w


def _silu(x):
    return x * jax.nn.sigmoid(x)


def _pad_rope(w):
    z = jnp.zeros(w.shape[:-1] + (ROPE_HALF,), w.dtype)
    return jnp.concatenate([w[..., 0::2], z, w[..., 1::2], z], axis=-1)


def _unpad_rope(w):
    return jnp.stack([w[..., 0:ROPE_HALF], w[..., 2 * ROPE_HALF:3 * ROPE_HALF]], axis=-1).reshape(
        w.shape[:-1] + (QK_ROPE,))


class _Geom:
    def __init__(self, dec_b, dec_s, b, s, d):
        self.dec_b, self.dec_s, self.b, self.s, self.d = dec_b, dec_s, b, s, d
        self.ns = dec_b * dec_s
        self.np_ = b * s
        self.t = self.ns + self.np_

    def mod_row(self, layer, tile):
        def f(i):
            tok = i * tile
            return layer * MOD_ROWS + jnp.where(tok < self.ns, tok // self.dec_s, self.dec_b)
        return f


def _x_in(g, tm, x, idx=lambda i: i):
    d = g.d
    if not isinstance(x, tuple):
        return [pl.BlockSpec((tm, d), lambda *a: (idx(*a), 0))], (x,)
    n_s = g.ns // tm
    return [pl.BlockSpec((tm, d), lambda *a: (jnp.minimum(idx(*a), n_s - 1), 0)),
            pl.BlockSpec((tm, d), lambda *a: (jnp.maximum(idx(*a) - n_s, 0), 0))], x


def _x_load(x_refs, is_sample):
    if len(x_refs) == 1:
        return x_refs[0][...]
    return jnp.where(is_sample, x_refs[0][...], x_refs[1][...])


def _mod_body(c_ref, w_ref, b_ref, o_ref):
    s = _silu(c_ref[...]).astype(BF16)
    w = w_ref[0].astype(BF16)
    o_ref[0] = jnp.dot(s, w, preferred_element_type=F32) + b_ref[0]


def _modulation(cond, w_mod, b_mod):
    depth, d, n = w_mod.shape
    tn = 1024
    return pl.pallas_call(
        _mod_body,
        grid=(depth, n // tn),
        in_specs=[
            pl.BlockSpec((MOD_ROWS, d), lambda l, j: (0, 0)),
            pl.BlockSpec((1, d, tn), lambda l, j: (l, 0, j)),
            pl.BlockSpec((1, 1, tn), lambda l, j: (l, 0, j)),
        ],
        out_specs=pl.BlockSpec((1, MOD_ROWS, tn), lambda l, j: (l, 0, j)),
        out_shape=jax.ShapeDtypeStruct((depth, MOD_ROWS, n), F32),
        compiler_params=_cparams(("arbitrary", "arbitrary")),
        name="modulation",
    )(cond, w_mod, b_mod.reshape(depth, 1, n))


def _rope128(x, c, s):
    return x * c + pltpu.roll(x, 2 * ROPE_HALF, 1) * s


def _assemble_kv(kv, kpe, knw, rope, k_ref, vt_ref):
    hv = MLA_HEADS * QK_NOPE
    kpe_ss = jnp.sum(kpe * kpe, axis=-1, keepdims=True)
    kpe_w = kpe * knw[:, QK_NOPE:]
    if rope is not None:
        kpe_w = _rope128(kpe_w, rope[0], rope[1])
    for h in range(MLA_HEADS):
        kn = kv[:, h * QK_NOPE:(h + 1) * QK_NOPE]
        ss = jnp.sum(kn * kn, axis=-1, keepdims=True) + kpe_ss
        r = lax.rsqrt(ss * (1.0 / QK_HEAD) + EPS)
        k_ref[:, h * HEAD_PAD:h * HEAD_PAD + QK_NOPE] = (kn * r * knw[:, :QK_NOPE]).astype(BF16)
        k_ref[:, h * HEAD_PAD + QK_NOPE:(h + 1) * HEAD_PAD] = (kpe_w * r).astype(BF16)
    vt_ref[...] = kv[:, hv:].T.astype(BF16)


def _proj_body(d, cc, n_s_tiles, nx, *refs):
    x_refs = refs[:nx]
    (mod_ref, n1w_ref, win_ref, qanw_ref, wqb_ref, kvanw_ref, wkvb_ref, qnw_ref, knw_ref, rc_ref, rs_ref,
     glu_ref, q_ref, k_ref, v_ref, ckv_ref, kpe_ref) = refs[nx:]
    m = mod_ref[0]
    x = _x_load(x_refs, pl.program_id(0) < n_s_tiles)
    h = _rms(x, n1w_ref[...]) * (1.0 + m[:, d:2 * d]) + m[:, 0:d]
    proj = jnp.dot(h.astype(BF16), win_ref[...], preferred_element_type=F32)
    o1 = 2 * cc
    o2 = o1 + Q_LORA
    o3 = o2 + KV_LORA
    glu_ref[...] = proj[:, :cc] * jax.nn.sigmoid(proj[:, cc:o1])

    rc = rc_ref[...]
    rs = rs_ref[...]
    qa = _rms(proj[:, o1:o2], qanw_ref[...]).astype(BF16)
    qb = jnp.dot(qa, wqb_ref[...], preferred_element_type=F32)
    qnw = qnw_ref[...]
    for hd in range(MLA_HEADS):
        qh = qb[:, hd * HEAD_PAD:(hd + 1) * HEAD_PAD]
        ss = jnp.sum(qh * qh, axis=-1, keepdims=True)
        r = lax.rsqrt(ss * (1.0 / QK_HEAD) + EPS)
        qn = qh * r * qnw
        q_ref[:, hd * HEAD_PAD:hd * HEAD_PAD + QK_NOPE] = qn[:, :QK_NOPE].astype(BF16)
        q_ref[:, hd * HEAD_PAD + QK_NOPE:(hd + 1) * HEAD_PAD] = _rope128(qn[:, QK_NOPE:], rc, rs).astype(BF16)

    ckv = _rms(proj[:, o2:o3], kvanw_ref[...])
    ckv_ref[...] = ckv
    kpe = proj[:, o3:]
    kpe_ref[...] = kpe
    kv = jnp.dot(ckv.astype(BF16), wkvb_ref[...], preferred_element_type=F32)
    _assemble_kv(kv, kpe, knw_ref[...], (rc, rs), k_ref, v_ref)


def _proj(g, layer, x, mod, n1w, win, qanw, wqb, kvanw, wkvb, qnw, knw, rope_c, rope_s, tm):
    d = g.d
    cc = d // 2
    nin = win.shape[1]
    nq = MLA_HEADS * HEAD_PAD
    nv = MLA_HEADS * V_HEAD
    s_tiles = g.dec_s // tm

    def rope_idx(i):
        return (jnp.where(i * tm < g.ns, i % s_tiles, s_tiles), 0)

    row = lambda n: pl.BlockSpec((tm, n), lambda i: (i, 0))
    mrow = g.mod_row(layer, tm)
    x_specs, xs = _x_in(g, tm, x)
    return pl.pallas_call(
        functools.partial(_proj_body, d, cc, g.ns // tm, len(xs)),
        grid=(g.t // tm,),
        in_specs=x_specs + [
            pl.BlockSpec((1, 1, N_MOD * d), lambda i: (mrow(i), 0, 0)),
            _const_spec((1, d)),
            _const_spec((d, nin)),
            _const_spec((1, Q_LORA)),
            _const_spec((Q_LORA, nq)),
            _const_spec((1, KV_LORA)),
            _const_spec((KV_LORA, nq)),
            _const_spec((1, HEAD_PAD)),
            _const_spec((1, HEAD_PAD)),
            pl.BlockSpec((tm, LANES), rope_idx),
            pl.BlockSpec((tm, LANES), rope_idx),
        ],
        out_specs=[row(cc), row(nq), row(nq), pl.BlockSpec((nv, tm), lambda i: (0, i)), row(KV_LORA),
                   row(LANES)],
        out_shape=[
            jax.ShapeDtypeStruct((g.t, cc), F32),
            jax.ShapeDtypeStruct((g.t, nq), BF16),
            jax.ShapeDtypeStruct((g.t, nq), BF16),
            jax.ShapeDtypeStruct((nv, g.t), BF16),
            jax.ShapeDtypeStruct((g.t, KV_LORA), F32),
            jax.ShapeDtypeStruct((g.t, LANES), F32),
        ],
        compiler_params=_cparams(("arbitrary",)),
        name="even_proj",
    )(*xs, mod, n1w, win, qanw, wqb, kvanw, wkvb, qnw, knw, rope_c, rope_s)


def _ctx_kv_body(ckv_ref, kpe_ref, wkvb_ref, knw_ref, k_ref, v_ref):
    kv = jnp.dot(ckv_ref[...], wkvb_ref[...], preferred_element_type=F32)
    _assemble_kv(kv, kpe_ref[...], knw_ref[...], None, k_ref, v_ref)


def _ctx_kv(ckv, kpe, wkvb, knw, tm):
    n = ckv.shape[0]
    nq = MLA_HEADS * HEAD_PAD
    nv = MLA_HEADS * V_HEAD
    row = lambda w: pl.BlockSpec((tm, w), lambda i: (i, 0))
    return pl.pallas_call(
        _ctx_kv_body,
        grid=(n // tm,),
        in_specs=[row(KV_LORA), row(LANES), _const_spec((KV_LORA, nq)), _const_spec((1, HEAD_PAD))],
        out_specs=[row(nq), pl.BlockSpec((nv, tm), lambda i: (0, i))],
        out_shape=[jax.ShapeDtypeStruct((n, nq), BF16), jax.ShapeDtypeStruct((nv, n), BF16)],
        compiler_params=_cparams(("arbitrary",)),
        name="ctx_kv",
    )(ckv, kpe, wkvb, knw)


_NT = (((1,), (1,)), ((), ()))
_EXP2_SCALE = math.log2(math.e) * QK_HEAD ** -0.5

ATTN_SUB = 512
ATTN_KCHUNK = 1024


def _attn_sample_body(q_ref, kc_ref, kl_ref, vct_ref, vlt_ref, o_ref, s0_ref, s1_ref, vt_ref):
    past = kc_ref.shape[0]
    nk = kl_ref.shape[0]
    sub = s0_ref.shape[1]
    nsub = q_ref.shape[0] // sub

    vt_ref[0:V_HEAD, 0:past] = vct_ref[...]
    vt_ref[0:V_HEAD, past:] = vlt_ref[...]
    vt_ref[V_HEAD:, :] = jnp.ones((ONES_ROWS, past + nk), BF16)

    def rows(i):
        return pl.ds(pl.multiple_of(i * sub, sub), sub)

    def scores(i, s_ref):
        q = q_ref[rows(i), :]
        s_ref[0:past, :] = lax.dot_general(kc_ref[...], q, _NT, preferred_element_type=F32)
        s_ref[past:, :] = lax.dot_general(kl_ref[...], q, _NT, preferred_element_type=F32)

    kchunk = min(ATTN_KCHUNK, nk)
    bounds = [0] + list(range(past, past + nk + 1, kchunk))

    def finish(i, s_ref):
        m = jnp.max(s_ref[...], axis=0, keepdims=True)
        o = None
        for lo, hi in zip(bounds[:-1], bounds[1:]):
            p = jnp.exp2(s_ref[lo:hi, :] - m).astype(BF16)
            part = jnp.dot(vt_ref[:, lo:hi], p, preferred_element_type=F32)
            o = part if o is None else o + part
        o_ref[rows(i), :] = (o[:V_HEAD] / o[V_HEAD:V_HEAD + 1]).T.astype(BF16)

    scores(0, s0_ref)

    def pair(t, carry):
        i = 2 * t
        scores(i + 1, s1_ref)
        finish(i, s0_ref)
        scores(i + 2, s0_ref)
        finish(i + 1, s1_ref)
        return carry

    lax.fori_loop(0, nsub // 2 - 1, pair, 0)
    scores(nsub - 1, s1_ref)
    finish(nsub - 2, s0_ref)
    finish(nsub - 1, s1_ref)


def _attn_sample(g, q, k, vt, kc, vct, past):
    sub = min(ATTN_SUB, g.dec_s // 2)
    assert g.dec_s % (2 * sub) == 0 and g.dec_s % min(ATTN_KCHUNK, g.dec_s) == 0
    nkeys = past + g.dec_s
    score_buf = pltpu.VMEM((nkeys, sub), F32)
    return pl.pallas_call(
        _attn_sample_body,
        grid=(g.dec_b, MLA_HEADS),
        in_specs=[
            pl.BlockSpec((g.dec_s, HEAD_PAD), lambda b, h: (b, h)),
            pl.BlockSpec((past, HEAD_PAD), lambda b, h: (b, h)),
            pl.BlockSpec((g.dec_s, HEAD_PAD), lambda b, h: (b, h)),
            pl.BlockSpec((V_HEAD, past), lambda b, h: (h, b)),
            pl.BlockSpec((V_HEAD, g.dec_s), lambda b, h: (h, b)),
        ],
        out_specs=pl.BlockSpec((g.dec_s, V_HEAD), lambda b, h: (b, h)),
        out_shape=jax.ShapeDtypeStruct((g.ns, MLA_HEADS * V_HEAD), BF16),
        scratch_shapes=[score_buf, score_buf, pltpu.VMEM((V_HEAD + ONES_ROWS, nkeys), BF16)],
        compiler_params=_cparams(("arbitrary", "arbitrary")),
        name="attn_sample",
    )(q, kc, k, vct, vt)


def _attn_prompt_body(q_ref, k_ref, vt_ref, o_ref):
    for h in range(MLA_HEADS):
        qk = slice(h * HEAD_PAD, (h + 1) * HEAD_PAD)
        s = lax.dot_general(q_ref[:, qk], k_ref[:, qk], _NT, preferred_element_type=F32)
        m = jnp.max(s, axis=-1, keepdims=True)
        p = jnp.exp2(s - m).astype(BF16)
        l = jnp.sum(p.astype(F32), axis=-1, keepdims=True)
        o = lax.dot_general(p, vt_ref[h * V_HEAD:(h + 1) * V_HEAD, :], _NT, preferred_element_type=F32)
        o_ref[:, h * V_HEAD:(h + 1) * V_HEAD] = (o / l).astype(BF16)


def _attn_prompt(g, q, k, vt):
    off = g.ns // g.s
    return pl.pallas_call(
        _attn_prompt_body,
        grid=(g.b,),
        in_specs=[
            pl.BlockSpec((g.s, MLA_HEADS * HEAD_PAD), lambda b: (off + b, 0)),
            pl.BlockSpec((g.s, MLA_HEADS * HEAD_PAD), lambda b: (off + b, 0)),
            pl.BlockSpec((MLA_HEADS * V_HEAD, g.s), lambda b: (0, off + b)),
        ],
        out_specs=pl.BlockSpec((g.s, MLA_HEADS * V_HEAD), lambda b: (b, 0)),
        out_shape=jax.ShapeDtypeStruct((g.np_, MLA_HEADS * V_HEAD), BF16),
        compiler_params=_cparams(("arbitrary",)),
        name="attn_prompt",
    )(q, k, vt)


def _chunk_pos(g, ch):
    n = pl.program_id(0)
    n_s = g.ns // ch
    cps_s = g.dec_s // ch
    cps_p = g.s // ch
    is_s = n < n_s
    j = jnp.where(is_s, n % cps_s, (n - n_s) % cps_p)
    cps = jnp.where(is_s, cps_s, cps_p)
    return j, cps


def _halo_specs(g, ch, halo, width):
    per = ch // halo
    last = g.t // halo - 1
    main = pl.BlockSpec((ch, width), lambda n: (n, 0))
    left = pl.BlockSpec((halo, width), lambda n: (jnp.maximum(n * per - 1, 0), 0))
    right = pl.BlockSpec((halo, width), lambda n: (jnp.minimum((n + 1) * per, last), 0))
    return main, left, right


CONV_ROWS = 32


def _conv_out_body(g, ch, nx, *refs):
    x_refs = refs[:nx]
    (glu_ref, gl_ref, gr_ref, w_ref, b_ref, lnw_ref, lnb_ref, mod_ref, as_ref, ap_ref, wo_ref,
     o_ref, xp_ref, xs_ref, ca_ref) = refs[nx:]
    d = g.d
    cc = glu_ref.shape[1]
    is_sample = pl.program_id(0) < g.ns // ch
    g1 = mod_ref[0][:, 2 * d:3 * d]

    j, cps = _chunk_pos(g, ch)
    xp_ref[0:CONV_HALO] = jnp.where(j > 0, gl_ref[...], 0.0)
    xp_ref[CONV_HALO:CONV_HALO + ch] = glu_ref[...]
    xp_ref[CONV_HALO + ch:2 * CONV_HALO + ch] = jnp.where(j < cps - 1, gr_ref[...], 0.0)
    bias = b_ref[...]
    lnw = lnw_ref[...]
    lnb = lnb_ref[...]
    base = CONV_HALO - CONV_K // 2

    span = xs_ref.shape[1]
    for ph in range(1, SUBLANES):
        xs_ref[ph] = xp_ref[ph:ph + span, :]

    def src(lo, ph):
        return xp_ref[lo:lo + SUBLANES, :] if ph == 0 else xs_ref[ph, lo:lo + SUBLANES, :]

    for r0 in range(0, ch, CONV_ROWS):
        accs = [None] * (CONV_ROWS // SUBLANES)
        for k in range(CONV_K):
            wk = w_ref[k]
            ph = (base + k) % SUBLANES
            for jj in range(len(accs)):
                term = src(r0 + jj * SUBLANES + base + k - ph, ph) * wk
                accs[jj] = term if accs[jj] is None else accs[jj] + term
        acc = jnp.concatenate(accs, axis=0) + bias
        mu = jnp.mean(acc, axis=-1, keepdims=True)
        xc = acc - mu
        var = jnp.mean(xc * xc, axis=-1, keepdims=True)
        y = xc * lax.rsqrt(var + EPS) * lnw + lnb
        ca_ref[r0:r0 + CONV_ROWS, :] = _silu(y).astype(BF16)

    attn = jnp.where(is_sample, as_ref[...], ap_ref[...])
    y = jnp.dot(ca_ref[...], wo_ref[0:cc, :], preferred_element_type=F32)
    y = y + jnp.dot(attn, wo_ref[cc:, :], preferred_element_type=F32)
    o_ref[...] = _x_load(x_refs, is_sample) + g1 * y


def _conv_out(g, layer, x, mod, glu, w, b, lnw, lnb, attn_s, attn_p, wo, ch):
    d = g.d
    cc = glu.shape[1]
    na = attn_s.shape[1]
    n_s = g.ns // ch
    main, left, right = _halo_specs(g, ch, CONV_HALO, cc)
    mrow = g.mod_row(layer, ch)
    x_specs, xs = _x_in(g, ch, x)
    return pl.pallas_call(
        functools.partial(_conv_out_body, g, ch, len(xs)),
        grid=(g.t // ch,),
        in_specs=x_specs + [
            main, left, right, _const_spec((CONV_K, SUBLANES, cc)), _const_spec((1, cc)),
            _const_spec((1, cc)), _const_spec((1, cc)),
            pl.BlockSpec((1, 1, N_MOD * d), lambda n: (mrow(n), 0, 0)),
            pl.BlockSpec((ch, na), lambda n: (jnp.minimum(n, n_s - 1), 0)),
            pl.BlockSpec((ch, na), lambda n: (jnp.maximum(n - n_s, 0), 0)),
            _const_spec((cc + na, d)),
        ],
        out_specs=pl.BlockSpec((ch, d), lambda n: (n, 0)),
        out_shape=jax.ShapeDtypeStruct((g.t, d), F32),
        scratch_shapes=[pltpu.VMEM((ch + 2 * CONV_HALO, cc), F32),
                        pltpu.VMEM((SUBLANES, ch + 2 * CONV_HALO - SUBLANES, cc), F32),
                        pltpu.VMEM((ch, cc), BF16)],
        compiler_params=_cparams(("arbitrary",)),
        name="conv_out",
    )(*xs, glu, glu, glu, w, b, lnw, lnb, mod, attn_s, attn_p, wo)


def _pool_body(g, ch, x_ref, xl_ref, xr_ref, mod_ref, n1w_ref, pw_ref, ps_ref, o_ref, hb_ref, mx_ref,
               *lvl_refs):
    d = g.d
    pg = d // len(POOL_WINDOWS)
    j, cps = _chunk_pos(g, ch)
    m = mod_ref[0]
    w = n1w_ref[...] * (1.0 + m[:, d:2 * d])
    shift = m[:, 0:d]

    def norm(x):
        return _rms(x, w) + shift

    pad = POOL_HALO
    base = pad + POOL_HALO
    total = base + ch + POOL_HALO
    hb_ref[0:pad] = jnp.zeros((pad, d), F32)
    hb_ref[pad:base] = jnp.where(j > 0, norm(xl_ref[...]), 0.0)
    hb_ref[base:base + ch] = norm(x_ref[...])
    hb_ref[base + ch:total] = jnp.where(j < cps - 1, norm(xr_ref[...]), 0.0)
    seq_len = cps * ch

    levels = (hb_ref,) + tuple(lvl_refs)
    for k in range(1, len(levels)):
        src, step = levels[k - 1], 1 << (k - 1)
        levels[k][0:pad] = jnp.zeros((pad, levels[k].shape[1]), F32)
        levels[k][pad:total] = src[pad:total, pg:] + src[pad - step:total - step, pg:]

    pos = j * ch + lax.broadcasted_iota(jnp.int32, (ch, 1), 0)
    for gi, w in enumerate(POOL_WINDOWS):
        half = w // 2
        assert half == 1 << gi
        cols = slice(gi * pg, (gi + 1) * pg)
        src = levels[gi]
        win = src[base - 1:base - 1 + ch, 0:pg] + src[base + half - 1:base + half - 1 + ch, 0:pg]
        cnt = jnp.minimum(pos + half, seq_len) - jnp.maximum(pos - half, 0)
        mx_ref[:, cols] = (win / cnt.astype(F32) - hb_ref[base:base + ch, cols]).astype(BF16)

    g1 = m[:, 2 * d:3 * d]
    ps = ps_ref[...]
    for gi in range(len(POOL_WINDOWS)):
        cols = slice(gi * pg, (gi + 1) * pg)
        y = jnp.dot(mx_ref[:, cols], pw_ref[gi], preferred_element_type=F32)
        o_ref[:, cols] = x_ref[:, cols] + g1[:, cols] * (y * ps[:, cols])


def _pool(g, layer, x, mod, n1w, pw, ps, ch):
    d = g.d
    pg = d // len(POOL_WINDOWS)
    main, left, right = _halo_specs(g, ch, POOL_HALO, d)
    mrow = g.mod_row(layer, ch)
    return pl.pallas_call(
        functools.partial(_pool_body, g, ch),
        grid=(g.t // ch,),
        in_specs=[main, left, right,
                  pl.BlockSpec((1, 1, N_MOD * d), lambda n: (mrow(n), 0, 0)),
                  _const_spec((1, d)),
                  _const_spec((len(POOL_WINDOWS), pg, pg)),
                  _const_spec((1, d))],
        out_specs=pl.BlockSpec((ch, d), lambda n: (n, 0)),
        out_shape=jax.ShapeDtypeStruct((g.t, d), F32),
        scratch_shapes=[pltpu.VMEM((ch + 3 * POOL_HALO, d), F32), pltpu.VMEM((ch, d), BF16)]
        + [pltpu.VMEM((ch + 3 * POOL_HALO, d - k * pg), F32) for k in range(1, len(POOL_WINDOWS))],
        compiler_params=_cparams(("arbitrary",)),
        name="pool_mixer",
    )(x, x, x, mod, n1w, pw, ps)


def _ffn_body(d, nj, x_ref, mod_ref, n2w_ref, wg_ref, wu_ref, wd_ref, o_ref, h_ref):
    j = pl.program_id(1)
    m = mod_ref[0]

    def gated_chunk(h):
        gate = jnp.dot(h, wg_ref[...], preferred_element_type=F32)
        up = jnp.dot(h, wu_ref[...], preferred_element_type=F32)
        a = (_silu(gate) * up).astype(BF16)
        return m[:, 5 * d:6 * d] * jnp.dot(a, wd_ref[...], preferred_element_type=F32)

    @pl.when(j == 0)
    def _():
        x = x_ref[...]
        h = (_rms(x, n2w_ref[...]) * (1.0 + m[:, 4 * d:5 * d]) + m[:, 3 * d:4 * d]).astype(BF16)
        h_ref[...] = h
        o_ref[...] = x + gated_chunk(h)

    @pl.when(j > 0)
    def _():
        o_ref[...] += gated_chunk(h_ref[...])


def _ffn(g, layer, x, mod, n2w, wg, wu, wd, tm, tf, tile0=0, ntiles=None):
    d = g.d
    dff = wg.shape[2]
    nj = dff // tf
    ntiles = g.t // tm if ntiles is None else ntiles
    mrow = g.mod_row(layer, tm)
    return pl.pallas_call(
        functools.partial(_ffn_body, d, nj),
        grid=(ntiles, nj),
        in_specs=[
            pl.BlockSpec((tm, d), lambda i, j: (tile0 + i, 0)),
            pl.BlockSpec((1, 1, N_MOD * d), lambda i, j: (mrow(tile0 + i), 0, 0)),
            pl.BlockSpec((1, d), lambda i, j: (0, 0)),
            pl.BlockSpec((None, d, tf), lambda i, j: (layer, 0, j)),
            pl.BlockSpec((None, d, tf), lambda i, j: (layer, 0, j)),
            pl.BlockSpec((None, tf, d), lambda i, j: (layer, j, 0)),
        ],
        out_specs=pl.BlockSpec((tm, d), lambda i, j: (i, 0)),
        out_shape=jax.ShapeDtypeStruct((ntiles * tm, d), F32),
        scratch_shapes=[pltpu.VMEM((tm, d), BF16)],
        compiler_params=_cparams(("arbitrary", "arbitrary")),
        name="ffn",
    )(x, mod, n2w, wg, wu, wd)


def _rope_tables(g, tm):
    rows = g.dec_s // GRID_W
    row = jnp.broadcast_to(jnp.arange(rows, dtype=F32)[:, None], (rows, GRID_W)).reshape(-1)
    col = jnp.broadcast_to(jnp.arange(GRID_W, dtype=F32)[None, :], (rows, GRID_W)).reshape(-1)
    n_freq = QK_ROPE // 4
    inv_freq = 1.0 / (ROPE_THETA ** (jnp.arange(n_freq, dtype=F32) / n_freq))
    ang = jnp.concatenate([row[:, None] * inv_freq, col[:, None] * inv_freq], axis=-1)
    cos, sin = jnp.cos(ang), jnp.sin(ang)
    z = jnp.zeros_like(cos)
    c = jnp.concatenate([cos, z, cos, z], axis=-1)
    s = jnp.concatenate([-sin, z, sin, z], axis=-1)
    c = jnp.concatenate([c, jnp.ones((tm, LANES), F32)], axis=0)
    s = jnp.concatenate([s, jnp.zeros((tm, LANES), F32)], axis=0)
    return c, s


def kernel(x_prompt, x_sample, cache_ckv, cache_kpe, c, c_ctx, norm1_w, norm2_w, w_mod, b_mod, w_in, conv_dw_w, conv_dw_b, conv_ln_w, conv_ln_b, q_a_norm_w, w_q_b, kv_a_norm_w, w_kv_b, q_norm_w, k_norm_w, w_out, pool_w, pool_scale, ffn_w_gate, ffn_w_up, ffn_w_down):
    b, s, d = x_prompt.shape
    dec_b, dec_s, _ = x_sample.shape
    past = cache_ckv.shape[2]
    depth = norm1_w.shape[0]
    g = _Geom(dec_b, dec_s, b, s, d)
    cc = d // 2
    assert dec_b + 1 <= MOD_ROWS and g.ns % s == 0 and dec_s % GRID_W == 0

    tm_proj = min(256, s)
    tm_ffn = min(1024, g.np_, dec_s)
    tf = 512
    ch = s

    cond = jnp.concatenate([c, c_ctx[None, :], jnp.zeros((MOD_ROWS - dec_b - 1, d), F32)], axis=0)
    mod = _modulation(cond, w_mod, b_mod).reshape(depth * MOD_ROWS, 1, N_MOD * d)

    rope_c, rope_s = _rope_tables(g, tm_proj)
    x = (x_sample.reshape(g.ns, d), x_prompt.reshape(g.np_, d))
    wg_all = ffn_w_gate.astype(BF16)
    wu_all = ffn_w_up.astype(BF16)
    wd_all = ffn_w_down.astype(BF16)

    new_ckv, new_kpe = [], []
    for layer in range(depth):
        n1w = norm1_w[layer][None, :]
        if layer % 2 == 0:
            e = layer // 2
            win = jnp.concatenate([w_in[e][:, :2 * cc + Q_LORA + KV_LORA],
                                   _pad_rope(w_in[e][:, 2 * cc + Q_LORA + KV_LORA:])], axis=-1).astype(BF16)
            wq = w_q_b[e].reshape(Q_LORA, MLA_HEADS, QK_HEAD)
            wqb = jnp.concatenate([wq[..., :QK_NOPE], _pad_rope(wq[..., QK_NOPE:])], axis=-1)
            wqb = wqb.reshape(Q_LORA, MLA_HEADS * HEAD_PAD).astype(BF16)
            wkv = w_kv_b[e].reshape(KV_LORA, MLA_HEADS, QK_NOPE + V_HEAD)
            wkvb = jnp.concatenate([wkv[..., :QK_NOPE].reshape(KV_LORA, -1),
                                    wkv[..., QK_NOPE:].reshape(KV_LORA, -1)], axis=-1).astype(BF16)
            qnw = jnp.concatenate([q_norm_w[e][:QK_NOPE], _pad_rope(q_norm_w[e][QK_NOPE:])])[None, :]
            qnw = qnw * _EXP2_SCALE
            knw = jnp.concatenate([k_norm_w[e][:QK_NOPE], _pad_rope(k_norm_w[e][QK_NOPE:])])[None, :]

            glu, q, k, v, ckv, kpe = _proj(
                g, layer, x, mod, n1w, win, q_a_norm_w[e][None, :], wqb, kv_a_norm_w[e][None, :], wkvb,
                qnw, knw, rope_c, rope_s, tm_proj)
            new_ckv.append(ckv[g.ns:].reshape(b, s, KV_LORA))
            new_kpe.append(_unpad_rope(kpe[g.ns:]).reshape(b, s, QK_ROPE))

            kc, vc = _ctx_kv(cache_ckv[:, e].reshape(dec_b * past, KV_LORA).astype(BF16),
                             _pad_rope(cache_kpe[:, e]).reshape(dec_b * past, LANES), wkvb, knw, past)
            attn_s = _attn_sample(g, q, k, v, kc, vc, past)
            attn_p = _attn_prompt(g, q, k, v)
            w_taps = jnp.broadcast_to(conv_dw_w[e][:, None, :], (CONV_K, SUBLANES, cc))
            x = _conv_out(g, layer, x, mod, glu, w_taps, conv_dw_b[e][None, :], conv_ln_w[e][None, :],
                          conv_ln_b[e][None, :], attn_s, attn_p, w_out[e].astype(BF16), ch)
        else:
            o = layer // 2
            x = _pool(g, layer, x, mod, n1w, pool_w[o].astype(BF16), pool_scale[o][None, :], ch)
        ffn = functools.partial(_ffn, g, layer, x, mod, norm2_w[layer][None, :], wg_all, wu_all, wd_all,
                                tm_ffn, tf)
        if layer < depth - 1:
            x = ffn()
        else:
            n_s_tiles = g.ns // tm_ffn
            x = (ffn(tile0=0, ntiles=n_s_tiles), ffn(tile0=n_s_tiles, ntiles=g.np_ // tm_ffn))

    y_sample = x[0].reshape(dec_b, dec_s, d)
    y_prompt = x[1].reshape(b, s, d)
    return (y_prompt, y_sample, jnp.stack(new_ckv, axis=1), jnp.stack(new_kpe, axis=1))
```
